```python
import jax, jax.numpy as jnp
from jax import lax
import numpy as np

D_MODEL = 2048
BATCH = 4
SEQ = 4096
DEPTH = 4

N_MIXERS = 3
EXPAND = 2
D_INNER = EXPAND * D_MODEL
CHUNK = 128
A_GROUPS = 8
A_GROUP_DIM = D_INNER // A_GROUPS
B_GROUPS = 8
B_GROUP_DIM = D_INNER // B_GROUPS
POOL_WINDOWS = (2, 4, 8, 16)
C_GROUPS = len(POOL_WINDOWS)
C_GROUP_DIM = D_INNER // C_GROUPS
EPS = 1e-6

kernel_name = "hybrid_gmlp_fnet_pool_encoder"


def _layers_of_kind(kind):
    return len(range(kind, DEPTH, N_MIXERS))


def rmsnorm(x, g):
    xf = x.astype(jnp.float32)
    y = xf * lax.rsqrt(jnp.mean(xf * xf, axis=-1, keepdims=True) + EPS)
    return (y * g.astype(jnp.float32)).astype(x.dtype)


def gmlp_mixer(h, w_in, v_gain, w_s, b_s):
    bsz, seq, _ = h.shape
    z = h @ w_in
    u, v, gate = jnp.split(z, 3, axis=-1)
    u = jax.nn.gelu(u)
    v = rmsnorm(jax.nn.gelu(v), v_gain)
    vc = v.reshape(bsz, seq // CHUNK, CHUNK, A_GROUPS, A_GROUP_DIM)
    sv = jnp.einsum('gpq,bnqgc->bnpgc', w_s, vc) + b_s.T[None, None, :, :, None]
    y = u * sv.reshape(bsz, seq, D_INNER)
    return y, gate


def fourier_mixer(h, w_in, w_mix):
    bsz, seq, _ = h.shape
    z = h @ w_in
    xb, gate = jnp.split(z, 2, axis=-1)
    xg = xb.reshape(bsz, seq, B_GROUPS, B_GROUP_DIM).astype(jnp.float32)
    f = jnp.fft.fft2(xg, axes=(1, 3), norm="ortho").real.astype(xb.dtype)
    y = jnp.einsum('bsgc,gcd->bsgd', f, w_mix).reshape(bsz, seq, D_INNER)
    return y, gate


def pool_mixer(h, w_in, w_mix, scale):
    bsz, seq, _ = h.shape
    z = h @ w_in
    xc, gate = jnp.split(z, 2, axis=-1)
    xg = xc.reshape(bsz, seq, C_GROUPS, C_GROUP_DIM)
    t = jnp.arange(seq)
    pooled = []
    for gi, w in enumerate(POOL_WINDOWS):
        xi = xg[:, :, gi, :].astype(jnp.float32)
        csum = jnp.pad(lax.cumsum(xi, axis=1), ((0, 0), (1, 0), (0, 0)))
        lo = jnp.clip(t - w // 2, 0, seq - 1)
        hi = jnp.clip(t + w - 1 - w // 2, 0, seq - 1)
        wsum = jnp.take(csum, hi + 1, axis=1) - jnp.take(csum, lo, axis=1)
        cnt = (hi - lo + 1).astype(jnp.float32)[None, :, None]
        pooled.append(wsum / cnt - xi)
    p = jnp.stack(pooled, axis=2).astype(xc.dtype)
    y = jnp.einsum('bsgc,gcd->bsgd', p, w_mix).reshape(bsz, seq, D_INNER) * scale
    return y, gate


def setup_inputs(seed: int = 0) -> dict:
    key = jax.random.key(seed)
    ks = jax.random.split(key, 20)
    na, nb, nc = _layers_of_kind(0), _layers_of_kind(1), _layers_of_kind(2)
    f32 = jnp.float32
    nrm = lambda k, shape, s: jax.random.normal(k, shape, f32) * s
    din = D_MODEL ** -0.5
    dout = D_INNER ** -0.5
    return {
        "x": jax.random.normal(ks[0], (BATCH, SEQ, D_MODEL), f32),
        "a_norm": 1.0 + nrm(ks[1], (na, D_MODEL), 0.05),
        "a_w_in": nrm(ks[2], (na, D_MODEL, 3 * D_INNER), din),
        "a_v_gain": 1.0 + nrm(ks[3], (na, D_INNER), 0.05),
        "a_w_s": nrm(ks[4], (na, A_GROUPS, CHUNK, CHUNK), 0.5 * CHUNK ** -0.5),
        "a_b_s": 1.0 + nrm(ks[5], (na, A_GROUPS, CHUNK), 0.1),
        "a_w_out": nrm(ks[6], (na, D_INNER, D_MODEL), dout),
        "b_norm": 1.0 + nrm(ks[7], (nb, D_MODEL), 0.05),
        "b_w_in": nrm(ks[8], (nb, D_MODEL, 2 * D_INNER), din),
        "b_w_mix": nrm(ks[9], (nb, B_GROUPS, B_GROUP_DIM, B_GROUP_DIM), B_GROUP_DIM ** -0.5),
        "b_w_out": nrm(ks[10], (nb, D_INNER, D_MODEL), dout),
        "c_norm": 1.0 + nrm(ks[11], (nc, D_MODEL), 0.05),
        "c_w_in": nrm(ks[12], (nc, D_MODEL, 2 * D_INNER), din),
        "c_w_mix": nrm(ks[13], (nc, C_GROUPS, C_GROUP_DIM, C_GROUP_DIM), C_GROUP_DIM ** -0.5),
        "c_scale": 1.0 + nrm(ks[14], (nc, D_INNER), 0.1),
        "c_w_out": nrm(ks[15], (nc, D_INNER, D_MODEL), dout),
        "final_norm": 1.0 + nrm(ks[16], (D_MODEL,), 0.05),
    }


def reference(x, a_norm, a_w_in, a_v_gain, a_w_s, a_b_s, a_w_out,
              b_norm, b_w_in, b_w_mix, b_w_out,
              c_norm, c_w_in, c_w_mix, c_scale, c_w_out, final_norm):
    for i in range(DEPTH):
        kind, j = i % N_MIXERS, i // N_MIXERS
        if kind == 0:
            h = rmsnorm(x, a_norm[j])
            y, gate = gmlp_mixer(h, a_w_in[j], a_v_gain[j], a_w_s[j], a_b_s[j])
            w_out = a_w_out[j]
        elif kind == 1:
            h = rmsnorm(x, b_norm[j])
            y, gate = fourier_mixer(h, b_w_in[j], b_w_mix[j])
            w_out = b_w_out[j]
        else:
            h = rmsnorm(x, c_norm[j])
            y, gate = pool_mixer(h, c_w_in[j], c_w_mix[j], c_scale[j])
            w_out = c_w_out[j]
        x = x + (y * jax.nn.silu(gate)) @ w_out
    return rmsnorm(x, final_norm)
```

```python
import functools

import jax
import jax.numpy as jnp
from jax import lax
from jax.experimental import pallas as pl
from jax.experimental.pallas import tpu as pltpu

D_MODEL = 2048
D_INNER = 4096
SEQ = 4096
CHUNK = 128
A_GROUPS = 8
A_GROUP_DIM = D_INNER // A_GROUPS
B_GROUPS = 8
B_GROUP_DIM = D_INNER // B_GROUPS
POOL_WINDOWS = (2, 4, 8, 16)
C_GROUPS = len(POOL_WINDOWS)
C_GROUP_DIM = D_INNER // C_GROUPS
EPS = 1e-6
HALO = 8

VMEM_LIMIT = 56 * 1024 * 1024

F32 = jnp.float32
BF16 = jnp.bfloat16


def _params(*sem):
    return pltpu.CompilerParams(dimension_semantics=sem, vmem_limit_bytes=VMEM_LIMIT)


def _norm_kernel(x_ref, g_ref, h_ref):
    x = x_ref[...]
    r = lax.rsqrt(jnp.mean(x * x, axis=-1, keepdims=True) + EPS)
    h_ref[...] = (x * r * g_ref[...]).astype(h_ref.dtype)


def rms_cast(x, g, tm=512):
    m, d = x.shape
    return pl.pallas_call(
        _norm_kernel,
        grid=(m // tm,),
        in_specs=[pl.BlockSpec((tm, d), lambda i: (i, 0)),
                  pl.BlockSpec((1, d), lambda i: (0, 0))],
        out_specs=pl.BlockSpec((tm, d), lambda i: (i, 0)),
        out_shape=jax.ShapeDtypeStruct((m, d), BF16),
        compiler_params=_params("parallel"),
        name="rms_cast",
    )(x, g.reshape(1, d))


def _proj_kernel(a_ref, w_ref, o_ref, *, act):
    acc = jnp.dot(a_ref[...], w_ref[...], preferred_element_type=F32)
    if act == "gelu":
        acc = jax.nn.gelu(acc)
    elif act == "silu":
        acc = jax.nn.silu(acc)
    o_ref[...] = acc.astype(o_ref.dtype)


def proj(a, w, col0, ncols, act, out_dtype, tm=1024, tn=1024):
    m, k = a.shape
    nb0 = col0 // tn
    return pl.pallas_call(
        functools.partial(_proj_kernel, act=act),
        grid=(ncols // tn, m // tm),
        in_specs=[pl.BlockSpec((tm, k), lambda j, i: (i, 0)),
                  pl.BlockSpec((k, tn), lambda j, i: (0, nb0 + j))],
        out_specs=pl.BlockSpec((tm, tn), lambda j, i: (i, j)),
        out_shape=jax.ShapeDtypeStruct((m, ncols), out_dtype),
        compiler_params=_params("parallel", "parallel"),
        name="proj_" + str(act),
    )(a, w)


def _out_kernel(y_ref, w_ref, x_ref, g_ref, *refs, n_tiles, tn, last):
    if last:
        o_ref, xrow = refs
    else:
        xo_ref, o_ref, xrow = refs
    n = pl.program_id(1)
    xn = x_ref[...] + jnp.dot(y_ref[...], w_ref[...], preferred_element_type=F32)
    if not last:
        xo_ref[...] = xn
    xrow[n] = xn

    @pl.when(n == n_tiles - 1)
    def _():
        ss = None
        for j in range(n_tiles):
            v = xrow[j]
            s = jnp.sum(v * v, axis=-1, keepdims=True)
            ss = s if ss is None else ss + s
        r = lax.rsqrt(ss * (1.0 / (n_tiles * tn)) + EPS)
        for j in range(n_tiles):
            cols = slice(j * tn, (j + 1) * tn)
            o_ref[:, cols] = (xrow[j] * r * g_ref[:, cols]).astype(o_ref.dtype)


def out_proj(y, w, x, g, last, tm=512, tn=1024):
    m, k = y.shape
    d = w.shape[1]
    n_tiles = d // tn
    in_specs = [pl.BlockSpec((tm, k), lambda i, n: (i, 0)),
                pl.BlockSpec((k, tn), lambda i, n: (0, n)),
                pl.BlockSpec((tm, tn), lambda i, n: (i, n)),
                pl.BlockSpec((1, d), lambda i, n: (0, 0))]
    row_spec = pl.BlockSpec((tm, d), lambda i, n: (i, 0))
    if last:
        out_specs = row_spec
        out_shape = jax.ShapeDtypeStruct((m, d), F32)
    else:
        out_specs = [pl.BlockSpec((tm, tn), lambda i, n: (i, n)), row_spec]
        out_shape = [jax.ShapeDtypeStruct((m, d), F32), jax.ShapeDtypeStruct((m, d), BF16)]
    return pl.pallas_call(
        functools.partial(_out_kernel, n_tiles=n_tiles, tn=tn, last=last),
        grid=(m // tm, n_tiles),
        in_specs=in_specs,
        out_specs=out_specs,
        out_shape=out_shape,
        scratch_shapes=[pltpu.VMEM((n_tiles, tm, tn), F32)],
        compiler_params=_params("parallel", "arbitrary"),
        name="out_proj",
    )(y, w, x, g.reshape(1, d))


def _amix_kernel(u_ref, gv_ref, sg_ref, gain_ref, ws_ref, bt_ref, o_ref, *, tm):
    gv = gv_ref[...].astype(F32)
    r = lax.rsqrt(jnp.mean(gv * gv, axis=-1, keepdims=True) + EPS)
    for g in range(A_GROUPS):
        cols = slice(g * A_GROUP_DIM, (g + 1) * A_GROUP_DIM)
        vn = (gv_ref[:, cols].astype(F32) * r * gain_ref[:, cols]).astype(BF16)
        wg = ws_ref[g]
        bias = bt_ref[:, g:g + 1]
        for c in range(tm // CHUNK):
            rows = slice(c * CHUNK, (c + 1) * CHUNK)
            sv = jnp.dot(wg, vn[rows], preferred_element_type=F32) + bias
            y = u_ref[rows, cols].astype(F32) * sv * sg_ref[rows, cols].astype(F32)
            o_ref[rows, cols] = y.astype(o_ref.dtype)


def amix(u, gv, sg, gain, w_s, b_s, tm=512):
    m, e = u.shape
    tok = pl.BlockSpec((tm, e), lambda i: (i, 0))
    return pl.pallas_call(
        functools.partial(_amix_kernel, tm=tm),
        grid=(m // tm,),
        in_specs=[tok, tok, tok,
                  pl.BlockSpec((1, e), lambda i: (0, 0)),
                  pl.BlockSpec((A_GROUPS, CHUNK, CHUNK), lambda i: (0, 0, 0)),
                  pl.BlockSpec((CHUNK, A_GROUPS), lambda i: (0, 0))],
        out_specs=tok,
        out_shape=jax.ShapeDtypeStruct((m, e), BF16),
        compiler_params=_params("parallel"),
        name="amix",
    )(u, gv, sg, gain.reshape(1, e), w_s.astype(BF16), b_s.T)


def _dft_tables(n):
    idx = jnp.arange(n, dtype=jnp.int32)
    ang = ((idx[:, None] * idx[None, :]) % n).astype(F32) * (2.0 * jnp.pi / n)
    return jnp.cos(ang), jnp.sin(ang)


def _fold_kernel(cs_ref, w_ref, o_ref, *, scale):
    acc = jnp.dot(cs_ref[...], w_ref[...].astype(BF16), preferred_element_type=F32)
    o_ref[...] = (acc * scale).astype(o_ref.dtype)


def fold_channel_dft(w_mix):
    g, c, _ = w_mix.shape
    cc, sc = _dft_tables(c)
    cs = jnp.concatenate([cc, sc], axis=0).astype(BF16)
    scale = float((SEQ * c) ** -0.5)
    return pl.pallas_call(
        functools.partial(_fold_kernel, scale=scale),
        grid=(g,),
        in_specs=[pl.BlockSpec((2 * c, c), lambda i: (0, 0)),
                  pl.BlockSpec((None, c, c), lambda i: (i, 0, 0))],
        out_specs=pl.BlockSpec((None, 2 * c, c), lambda i: (i, 0, 0)),
        out_shape=jax.ShapeDtypeStruct((g, 2 * c, c), BF16),
        compiler_params=_params("parallel"),
        name="fold_channel_dft",
    )(cs, w_mix)


def _seq_dft_kernel(f_ref, x_ref, o_ref):
    o_ref[...] = jnp.dot(f_ref[...], x_ref[...], preferred_element_type=F32).astype(o_ref.dtype)


def seq_dft(xb, bsz, tm=1024, tn=1024):
    e = xb.shape[1]
    cs, ss = _dft_tables(SEQ)
    fmat = jnp.concatenate([cs, -ss], axis=0).astype(BF16)
    out = pl.pallas_call(
        _seq_dft_kernel,
        grid=(bsz, e // tn, 2 * SEQ // tm),
        in_specs=[pl.BlockSpec((tm, SEQ), lambda b, n, i: (i, 0)),
                  pl.BlockSpec((None, SEQ, tn), lambda b, n, i: (b, 0, n))],
        out_specs=pl.BlockSpec((None, tm, tn), lambda b, n, i: (b, i, n)),
        out_shape=jax.ShapeDtypeStruct((bsz, 2 * SEQ, e), BF16),
        compiler_params=_params("parallel", "parallel", "parallel"),
        name="seq_dft",
    )(fmat, xb.reshape(bsz, SEQ, e))
    return out.reshape(bsz, 2, SEQ, e)


def _bmix_kernel(yr_ref, yi_ref, w_ref, sg_ref, o_ref):
    c = B_GROUP_DIM
    acc = jnp.dot(yr_ref[...], w_ref[:c, :], preferred_element_type=F32)
    acc += jnp.dot(yi_ref[...], w_ref[c:, :], preferred_element_type=F32)
    o_ref[...] = (acc * sg_ref[...].astype(F32)).astype(o_ref.dtype)


def bmix(yri, wf, sg, tm=1024):
    bsz, _, s, e = yri.shape
    c = B_GROUP_DIM
    spb = s // tm
    return pl.pallas_call(
        _bmix_kernel,
        grid=(e // c, bsz * spb),
        in_specs=[pl.BlockSpec((None, None, tm, c), lambda g, i: (i // spb, 0, i % spb, g)),
                  pl.BlockSpec((None, None, tm, c), lambda g, i: (i // spb, 1, i % spb, g)),
                  pl.BlockSpec((None, 2 * c, c), lambda g, i: (g, 0, 0)),
                  pl.BlockSpec((tm, c), lambda g, i: (i, g))],
        out_specs=pl.BlockSpec((tm, c), lambda g, i: (i, g)),
        out_shape=jax.ShapeDtypeStruct((bsz * s, e), BF16),
        compiler_params=_params("parallel", "parallel"),
        name="bmix",
    )(yri, yri, wf, sg)


def _cmix_kernel(prev_ref, cur_ref, next_ref, w_ref, scale_ref, sg_ref, o_ref, p_ref, *, tm):
    g = pl.program_id(0)
    s_idx = pl.program_id(1) % (SEQ // tm)
    n_ext = tm + 2 * HALO
    prev = jnp.where(s_idx == 0, 0.0, prev_ref[...])
    nxt = jnp.where(s_idx == SEQ // tm - 1, 0.0, next_ref[...])
    cur = cur_ref[...]
    ext = jnp.concatenate([prev, cur, nxt], axis=0)
    t = s_idx * tm + lax.broadcasted_iota(jnp.int32, (tm, 1), 0)

    def back(a, d):
        return pltpu.roll(a, d, 0)

    def fwd(a, d):
        return pltpu.roll(a, n_ext - d, 0)

    def pooled(w):
        s = ext + back(ext, 1)
        half = 1
        while 2 * half < w:
            s = back(s, half) + fwd(s, half)
            half *= 2
        lo = jnp.maximum(t - w // 2, 0)
        hi = jnp.minimum(t + (w - 1 - w // 2), SEQ - 1)
        cnt = (hi - lo + 1).astype(F32)
        return s[HALO:HALO + tm] / cnt - cur

    for gi, w in enumerate(POOL_WINDOWS):
        @pl.when(g == gi)
        def _(w=w):
            p_ref[...] = pooled(w).astype(p_ref.dtype)

    acc = jnp.dot(p_ref[...], w_ref[...], preferred_element_type=F32)
    o_ref[...] = (acc * scale_ref[...] * sg_ref[...].astype(F32)).astype(o_ref.dtype)


def cmix(xc, w_mix, scale, sg, tm=512):
    m, e = xc.shape
    c = C_GROUP_DIM
    hb = tm // HALO
    last_hb = m // HALO - 1
    return pl.pallas_call(
        functools.partial(_cmix_kernel, tm=tm),
        grid=(C_GROUPS, m // tm),
        in_specs=[pl.BlockSpec((HALO, c), lambda g, i: (jnp.maximum(i * hb - 1, 0), g)),
                  pl.BlockSpec((tm, c), lambda g, i: (i, g)),
                  pl.BlockSpec((HALO, c), lambda g, i: (jnp.minimum((i + 1) * hb, last_hb), g)),
                  pl.BlockSpec((None, c, c), lambda g, i: (g, 0, 0)),
                  pl.BlockSpec((1, c), lambda g, i: (0, g)),
                  pl.BlockSpec((tm, c), lambda g, i: (i, g))],
        out_specs=pl.BlockSpec((tm, c), lambda g, i: (i, g)),
        out_shape=jax.ShapeDtypeStruct((m, e), BF16),
        scratch_shapes=[pltpu.VMEM((tm, c), BF16)],
        compiler_params=_params("parallel", "parallel"),
        name="cmix",
    )(xc, xc, xc, w_mix, scale.reshape(1, e), sg)


def kernel(x, a_norm, a_w_in, a_v_gain, a_w_s, a_b_s, a_w_out, b_norm, b_w_in, b_w_mix, b_w_out,
           c_norm, c_w_in, c_w_mix, c_scale, c_w_out, final_norm):
    bsz, seq, d = x.shape
    assert (seq, d) == (SEQ, D_MODEL)
    e = D_INNER
    depth = 4
    norms = {0: a_norm, 1: b_norm, 2: c_norm}

    def gain_for(i):
        if i == depth:
            return final_norm
        return norms[i % 3][i // 3]

    xf = x.reshape(bsz * seq, d)
    h = rms_cast(xf, gain_for(0))
    for i in range(depth):
        kind, j = i % 3, i // 3
        if kind == 0:
            w_in = a_w_in[j].astype(BF16)
            u = proj(h, w_in, 0, e, "gelu", BF16)
            gv = proj(h, w_in, e, e, "gelu", BF16)
            sg = proj(h, w_in, 2 * e, e, "silu", BF16)
            y = amix(u, gv, sg, a_v_gain[j], a_w_s[j], a_b_s[j])
            w_out = a_w_out[j]
        elif kind == 1:
            w_in = b_w_in[j].astype(BF16)
            xb = proj(h, w_in, 0, e, None, BF16)
            sg = proj(h, w_in, e, e, "silu", BF16)
            yri = seq_dft(xb, bsz)
            wf = fold_channel_dft(b_w_mix[j])
            y = bmix(yri, wf, sg)
            w_out = b_w_out[j]
        else:
            w_in = c_w_in[j].astype(BF16)
            xc = proj(h, w_in, 0, e, None, F32)
            sg = proj(h, w_in, e, e, "silu", BF16)
            y = cmix(xc, c_w_mix[j].astype(BF16), c_scale[j], sg)
            w_out = c_w_out[j]
        last = i == depth - 1
        res = out_proj(y, w_out.astype(BF16), xf, gain_for(i + 1), last)
        if last:
            return res.reshape(bsz, seq, d)
        xf, h = res
```

```python
import cmath
import functools
import math

import numpy as np
import jax
import jax.numpy as jnp
from jax import lax
from jax.experimental import pallas as pl
from jax.experimental.pallas import tpu as pltpu

D_MODEL = 2048
D_INNER = 4096
SEQ = 4096
CHUNK = 128
A_GROUPS = 8
A_GROUP_DIM = D_INNER // A_GROUPS
B_GROUPS = 8
B_GROUP_DIM = D_INNER // B_GROUPS
POOL_WINDOWS = (2, 4, 8, 16)
C_GROUPS = len(POOL_WINDOWS)
C_GROUP_DIM = D_INNER // C_GROUPS
EPS = 1e-6
HALO = 8

FFT_SLABS = 16
FFT_SLAB = SEQ // FFT_SLABS
BF16_ROWS = 16
LANES = 128

VMEM_LIMIT = 56 * 1024 * 1024

F32 = jnp.float32
BF16 = jnp.bfloat16


def _params(*sem):
    return pltpu.CompilerParams(dimension_semantics=sem, vmem_limit_bytes=VMEM_LIMIT)


def _norm_kernel(x_ref, g_ref, h_ref):
    x = x_ref[...]
    r = lax.rsqrt(jnp.mean(x * x, axis=-1, keepdims=True) + EPS)
    h_ref[...] = (x * r * g_ref[...]).astype(h_ref.dtype)


def rms_cast(x, g, tm=512):
    m, d = x.shape
    return pl.pallas_call(
        _norm_kernel,
        grid=(m // tm,),
        in_specs=[pl.BlockSpec((tm, d), lambda i: (i, 0)),
                  pl.BlockSpec((1, d), lambda i: (0, 0))],
        out_specs=pl.BlockSpec((tm, d), lambda i: (i, 0)),
        out_shape=jax.ShapeDtypeStruct((m, d), BF16),
        compiler_params=_params("parallel"),
        name="rms_cast",
    )(x, g.reshape(1, d))


PROJ_ROWS = 256


def _act(acc, act):
    if act == "gelu":
        return jax.nn.gelu(acc)
    if act == "silu":
        return jax.nn.silu(acc)
    return acc


def _proj_kernel(a_ref, w_ref, o_ref, wbf_ref, *, act, tm):
    @pl.when(pl.program_id(1) == 0)
    def _():
        wbf_ref[...] = w_ref[...].astype(BF16)

    for r in range(tm // PROJ_ROWS):
        rows = slice(r * PROJ_ROWS, (r + 1) * PROJ_ROWS)
        acc = jnp.dot(a_ref[rows, :], wbf_ref[...], preferred_element_type=F32)
        o_ref[rows, :] = _act(acc, act).astype(o_ref.dtype)


def proj(a, w, layer, col0, ncols, act, out_dtype, tm=1024, tn=1024):
    m, k = a.shape
    nb0 = col0 // tn
    return pl.pallas_call(
        functools.partial(_proj_kernel, act=act, tm=tm),
        grid=(ncols // tn, m // tm),
        in_specs=[pl.BlockSpec((tm, k), lambda j, i: (i, 0)),
                  pl.BlockSpec((None, k, tn), lambda j, i: (layer, 0, nb0 + j))],
        out_specs=pl.BlockSpec((tm, tn), lambda j, i: (i, j)),
        out_shape=jax.ShapeDtypeStruct((m, ncols), out_dtype),
        scratch_shapes=[pltpu.VMEM((k, tn), BF16)],
        compiler_params=_params("arbitrary", "arbitrary"),
        name="proj_" + str(act),
    )(a, w)


def _out_kernel(y_ref, w_ref, x_ref, g_ref, *refs, n_tiles, tn, last):
    if last:
        o_ref, xrow = refs
    else:
        xo_ref, o_ref, xrow = refs
    n = pl.program_id(1)
    xn = x_ref[...] + jnp.dot(y_ref[...], w_ref[...], preferred_element_type=F32)
    if not last:
        xo_ref[...] = xn
    xrow[n] = xn

    @pl.when(n == n_tiles - 1)
    def _():
        ss = None
        for j in range(n_tiles):
            v = xrow[j]
            s = jnp.sum(v * v, axis=-1, keepdims=True)
            ss = s if ss is None else ss + s
        r = lax.rsqrt(ss * (1.0 / (n_tiles * tn)) + EPS)
        for j in range(n_tiles):
            cols = slice(j * tn, (j + 1) * tn)
            o_ref[:, cols] = (xrow[j] * r * g_ref[:, cols]).astype(o_ref.dtype)


def out_proj(y, w, x, g, last, tm=1024, tn=512):
    m, k = y.shape
    d = w.shape[1]
    n_tiles = d // tn
    in_specs = [pl.BlockSpec((tm, k), lambda i, n: (i, 0)),
                pl.BlockSpec((k, tn), lambda i, n: (0, n)),
                pl.BlockSpec((tm, tn), lambda i, n: (i, n)),
                pl.BlockSpec((1, d), lambda i, n: (0, 0))]
    row_spec = pl.BlockSpec((tm, d), lambda i, n: (i, 0))
    if last:
        out_specs = row_spec
        out_shape = jax.ShapeDtypeStruct((m, d), F32)
    else:
        out_specs = [pl.BlockSpec((tm, tn), lambda i, n: (i, n)), row_spec]
        out_shape = [jax.ShapeDtypeStruct((m, d), F32), jax.ShapeDtypeStruct((m, d), BF16)]
    return pl.pallas_call(
        functools.partial(_out_kernel, n_tiles=n_tiles, tn=tn, last=last),
        grid=(m // tm, n_tiles),
        in_specs=in_specs,
        out_specs=out_specs,
        out_shape=out_shape,
        scratch_shapes=[pltpu.VMEM((n_tiles, tm, tn), F32)],
        compiler_params=_params("parallel", "arbitrary"),
        name="out_proj",
    )(y, w, x, g.reshape(1, d))


def _amix_kernel(u_ref, gv_ref, sg_ref, gain_ref, ws_ref, bt_ref, o_ref, *, tm):
    gv = gv_ref[...].astype(F32)
    r = lax.rsqrt(jnp.mean(gv * gv, axis=-1, keepdims=True) + EPS)
    for g in range(A_GROUPS):
        cols = slice(g * A_GROUP_DIM, (g + 1) * A_GROUP_DIM)
        vn = (gv_ref[:, cols].astype(F32) * r * gain_ref[:, cols]).astype(BF16)
        wg = ws_ref[g]
        bias = bt_ref[:, g:g + 1]
        for c in range(tm // CHUNK):
            rows = slice(c * CHUNK, (c + 1) * CHUNK)
            sv = jnp.dot(wg, vn[rows], preferred_element_type=F32) + bias
            y = u_ref[rows, cols].astype(F32) * sv * sg_ref[rows, cols].astype(F32)
            o_ref[rows, cols] = y.astype(o_ref.dtype)


def amix(u, gv, sg, gain, w_s, b_s, tm=512):
    m, e = u.shape
    tok = pl.BlockSpec((tm, e), lambda i: (i, 0))
    return pl.pallas_call(
        functools.partial(_amix_kernel, tm=tm),
        grid=(m // tm,),
        in_specs=[tok, tok, tok,
                  pl.BlockSpec((1, e), lambda i: (0, 0)),
                  pl.BlockSpec((A_GROUPS, CHUNK, CHUNK), lambda i: (0, 0, 0)),
                  pl.BlockSpec((CHUNK, A_GROUPS), lambda i: (0, 0))],
        out_specs=tok,
        out_shape=jax.ShapeDtypeStruct((m, e), BF16),
        compiler_params=_params("parallel"),
        name="amix",
    )(u, gv, sg, gain.reshape(1, e), w_s.astype(BF16), b_s.T)


def _dft_tables(n):
    idx = jnp.arange(n, dtype=jnp.int32)
    ang = ((idx[:, None] * idx[None, :]) % n).astype(F32) * (2.0 * jnp.pi / n)
    return jnp.cos(ang), jnp.sin(ang)


def _fold_kernel(cs_ref, w_ref, o_ref, *, scale):
    acc = jnp.dot(cs_ref[...], w_ref[...].astype(BF16), preferred_element_type=F32)
    o_ref[...] = (acc * scale).astype(o_ref.dtype)


def fold_channel_dft(w_mix, layer):
    _, g, c, _ = w_mix.shape
    cc, sc = _dft_tables(c)
    cs = jnp.concatenate([cc, sc], axis=0).astype(BF16)
    scale = float((SEQ * c) ** -0.5)
    return pl.pallas_call(
        functools.partial(_fold_kernel, scale=scale),
        grid=(g,),
        in_specs=[pl.BlockSpec((2 * c, c), lambda i: (0, 0)),
                  pl.BlockSpec((None, None, c, c), lambda i: (layer, i, 0, 0))],
        out_specs=pl.BlockSpec((None, 2 * c, c), lambda i: (i, 0, 0)),
        out_shape=jax.ShapeDtypeStruct((g, 2 * c, c), BF16),
        compiler_params=_params("parallel"),
        name="fold_channel_dft",
    )(cs, w_mix)


def _slab_dft_tables():
    k1 = jnp.arange(FFT_SLABS, dtype=jnp.int32)[:, None, None]
    j = jnp.arange(FFT_SLAB, dtype=jnp.int32)[None, :, None]
    n2 = jnp.arange(FFT_SLAB, dtype=jnp.int32)[None, None, :]
    ang = (((k1 + FFT_SLABS * j) * n2) % SEQ).astype(F32) * (2.0 * jnp.pi / SEQ)
    cos, sin = jnp.cos(ang), jnp.sin(ang)
    s = jnp.where(k1 <= FFT_SLABS // 2, 1.0, -1.0).astype(F32)
    top = jnp.concatenate([cos, s * sin], axis=2)
    bot = jnp.concatenate([-sin, s * cos], axis=2)
    return jnp.concatenate([top, bot], axis=1).astype(BF16)


def _interleave_table():
    r = np.arange(FFT_SLAB)
    p = np.zeros((FFT_SLAB, FFT_SLAB), np.float32)
    p[r, FFT_SLABS * (r % FFT_SLABS) + r // FFT_SLABS] = 1.0
    return jnp.asarray(p, dtype=BF16)


def _root(n, e):
    return cmath.exp(-2j * math.pi * (e % n) / n)


def _snap(v):
    for t in (0.0, 1.0, -1.0):
        if abs(v - t) < 1e-12:
            return t
    return v


def _radd(x, y):
    if x is None:
        return y
    if y is None:
        return x
    return x + y


def _rsub(x, y):
    if y is None:
        return x
    if x is None:
        return -y
    return x - y


def _rscale(x, c):
    if x is None or c == 0.0:
        return None
    if c == 1.0:
        return x
    if c == -1.0:
        return -x
    return x * c


def _cmul_const(a, w):
    wr, wi = _snap(w.real), _snap(w.imag)
    return (_rsub(_rscale(a[0], wr), _rscale(a[1], wi)),
            _radd(_rscale(a[0], wi), _rscale(a[1], wr)))


def _dft16_real(xs):
    stage = [[None] * 4 for _ in range(4)]
    for m2 in range(4):
        x0, x1, x2, x3 = xs[m2], xs[4 + m2], xs[8 + m2], xs[12 + m2]
        s0, s1, d0, d1 = x0 + x2, x1 + x3, x0 - x2, x1 - x3
        stage[0][m2] = (s0 + s1, None)
        stage[2][m2] = (s0 - s1, None)
        stage[1][m2] = (d0, -d1)
        stage[3][m2] = (d0, d1)
    out = [None] * (FFT_SLABS // 2 + 1)
    for q1 in range(4):
        tw = [_cmul_const(stage[q1][m2], _root(16, m2 * q1)) for m2 in range(4)]
        for q2 in range(4):
            k1 = q1 + 4 * q2
            if k1 > FFT_SLABS // 2:
                continue
            acc = (None, None)
            for m2 in range(4):
                t = _cmul_const(tw[m2], _root(4, m2 * q2))
                acc = (_radd(acc[0], t[0]), _radd(acc[1], t[1]))
            out[k1] = acc
    return out


def _fftmix_kernel(x_ref, g_ref, wf_ref, p_ref, sg_ref, o_ref, a_ref, ym_ref, *, tc):
    half = FFT_SLABS // 2
    c = B_GROUP_DIM

    def strip(i, carry):
        r = pl.multiple_of(i * BF16_ROWS, BF16_ROWS)
        for lc in range(tc // LANES):
            lanes = slice(lc * LANES, (lc + 1) * LANES)
            xs = [x_ref[pl.ds(m * FFT_SLAB + r, BF16_ROWS), lanes].astype(F32)
                  for m in range(FFT_SLABS)]
            for k1, (re, im) in enumerate(_dft16_real(xs)):
                if im is None:
                    im = jnp.zeros_like(re)
                a_ref[k1, pl.ds(r, BF16_ROWS), lanes] = re.astype(BF16)
                a_ref[k1, pl.ds(FFT_SLAB + r, BF16_ROWS), lanes] = im.astype(BF16)
        return carry

    lax.fori_loop(0, FFT_SLAB // BF16_ROWS, strip, 0)

    for k1 in range(FFT_SLABS):
        a = a_ref[k1 if k1 <= half else FFT_SLABS - k1]
        y = jnp.dot(g_ref[k1], a, preferred_element_type=F32).astype(BF16)
        for g in range(tc // c):
            cols = slice(g * c, (g + 1) * c)
            mix = jnp.dot(y[:FFT_SLAB, cols], wf_ref[g, :c, :], preferred_element_type=F32)
            mix += jnp.dot(y[FFT_SLAB:, cols], wf_ref[g, c:, :], preferred_element_type=F32)
            mix = mix.astype(BF16)
            for k3 in range(FFT_SLABS):
                ym_ref[k3, k1 * FFT_SLABS:(k1 + 1) * FFT_SLABS, cols] = (
                    mix[k3 * FFT_SLABS:(k3 + 1) * FFT_SLABS])

    for k3 in range(FFT_SLABS):
        rows = slice(k3 * FFT_SLAB, (k3 + 1) * FFT_SLAB)
        z = jnp.dot(p_ref[...], ym_ref[k3], preferred_element_type=F32)
        o_ref[rows, :] = (z * sg_ref[rows, :].astype(F32)).astype(o_ref.dtype)


def fftmix(xb, wf, sg, bsz, tc=512):
    e = xb.shape[1]
    gpt = tc // B_GROUP_DIM
    tile = pl.BlockSpec((None, SEQ, tc), lambda b, j: (b, 0, j))
    once = pl.Buffered(1)
    out = pl.pallas_call(
        functools.partial(_fftmix_kernel, tc=tc),
        grid=(bsz, e // tc),
        in_specs=[tile,
                  pl.BlockSpec((FFT_SLABS, 2 * FFT_SLAB, 2 * FFT_SLAB), lambda b, j: (0, 0, 0),
                               pipeline_mode=once),
                  pl.BlockSpec((gpt, 2 * B_GROUP_DIM, B_GROUP_DIM), lambda b, j: (j, 0, 0)),
                  pl.BlockSpec((FFT_SLAB, FFT_SLAB), lambda b, j: (0, 0), pipeline_mode=once),
                  tile],
        out_specs=tile,
        out_shape=jax.ShapeDtypeStruct((bsz, SEQ, e), BF16),
        scratch_shapes=[pltpu.VMEM((FFT_SLABS // 2 + 1, 2 * FFT_SLAB, tc), BF16),
                        pltpu.VMEM((FFT_SLABS, FFT_SLAB, tc), BF16)],
        compiler_params=_params("parallel", "parallel"),
        name="fftmix",
    )(xb.reshape(bsz, SEQ, e), _slab_dft_tables(), wf, _interleave_table(), sg.reshape(bsz, SEQ, e))
    return out.reshape(bsz * SEQ, e)


def _cmix_kernel(prev_ref, cur_ref, next_ref, w_ref, scale_ref, sg_ref, o_ref, p_ref, wbf_ref, *, tm):
    g = pl.program_id(0)

    @pl.when(pl.program_id(1) == 0)
    def _():
        wbf_ref[...] = w_ref[...].astype(BF16)

    s_idx = pl.program_id(1) % (SEQ // tm)
    n_ext = tm + 2 * HALO
    prev = jnp.where(s_idx == 0, 0.0, prev_ref[...])
    nxt = jnp.where(s_idx == SEQ // tm - 1, 0.0, next_ref[...])
    cur = cur_ref[...]
    ext = jnp.concatenate([prev, cur, nxt], axis=0)
    t = s_idx * tm + lax.broadcasted_iota(jnp.int32, (tm, 1), 0)

    def back(a, d):
        return pltpu.roll(a, d, 0)

    def fwd(a, d):
        return pltpu.roll(a, n_ext - d, 0)

    def pooled(w):
        s = ext + back(ext, 1)
        half = 1
        while 2 * half < w:
            s = back(s, half) + fwd(s, half)
            half *= 2
        lo = jnp.maximum(t - w // 2, 0)
        hi = jnp.minimum(t + (w - 1 - w // 2), SEQ - 1)
        cnt = (hi - lo + 1).astype(F32)
        return s[HALO:HALO + tm] / cnt - cur

    for gi, w in enumerate(POOL_WINDOWS):
        @pl.when(g == gi)
        def _(w=w):
            p_ref[...] = pooled(w).astype(p_ref.dtype)

    acc = jnp.dot(p_ref[...], wbf_ref[...], preferred_element_type=F32)
    o_ref[...] = (acc * scale_ref[...] * sg_ref[...].astype(F32)).astype(o_ref.dtype)


def cmix(xc, w_mix, layer, scale, sg, tm=512):
    m, e = xc.shape
    c = C_GROUP_DIM
    hb = tm // HALO
    last_hb = m // HALO - 1
    return pl.pallas_call(
        functools.partial(_cmix_kernel, tm=tm),
        grid=(C_GROUPS, m // tm),
        in_specs=[pl.BlockSpec((HALO, c), lambda g, i: (jnp.maximum(i * hb - 1, 0), g)),
                  pl.BlockSpec((tm, c), lambda g, i: (i, g)),
                  pl.BlockSpec((HALO, c), lambda g, i: (jnp.minimum((i + 1) * hb, last_hb), g)),
                  pl.BlockSpec((None, None, c, c), lambda g, i: (layer, g, 0, 0)),
                  pl.BlockSpec((1, c), lambda g, i: (0, g)),
                  pl.BlockSpec((tm, c), lambda g, i: (i, g))],
        out_specs=pl.BlockSpec((tm, c), lambda g, i: (i, g)),
        out_shape=jax.ShapeDtypeStruct((m, e), BF16),
        scratch_shapes=[pltpu.VMEM((tm, c), BF16), pltpu.VMEM((c, c), BF16)],
        compiler_params=_params("arbitrary", "arbitrary"),
        name="cmix",
    )(xc, xc, xc, w_mix, scale.reshape(1, e), sg)


def kernel(x, a_norm, a_w_in, a_v_gain, a_w_s, a_b_s, a_w_out, b_norm, b_w_in, b_w_mix, b_w_out,
           c_norm, c_w_in, c_w_mix, c_scale, c_w_out, final_norm):
    bsz, seq, d = x.shape
    assert (seq, d) == (SEQ, D_MODEL)
    e = D_INNER
    depth = 4
    norms = {0: a_norm, 1: b_norm, 2: c_norm}

    def gain_for(i):
        if i == depth:
            return final_norm
        return norms[i % 3][i // 3]

    xf = x.reshape(bsz * seq, d)
    h = rms_cast(xf, gain_for(0))
    for i in range(depth):
        kind, j = i % 3, i // 3
        if kind == 0:
            u = proj(h, a_w_in, j, 0, e, "gelu", BF16)
            gv = proj(h, a_w_in, j, e, e, "gelu", BF16)
            sg = proj(h, a_w_in, j, 2 * e, e, "silu", BF16)
            y = amix(u, gv, sg, a_v_gain[j], a_w_s[j], a_b_s[j])
            w_out = a_w_out[j]
        elif kind == 1:
            xb = proj(h, b_w_in, j, 0, e, None, BF16)
            sg = proj(h, b_w_in, j, e, e, "silu", BF16)
            wf = fold_channel_dft(b_w_mix, j)
            y = fftmix(xb, wf, sg, bsz)
            w_out = b_w_out[j]
        else:
            xc = proj(h, c_w_in, j, 0, e, None, F32)
            sg = proj(h, c_w_in, j, e, e, "silu", BF16)
            y = cmix(xc, c_w_mix, j, c_scale[j], sg)
            w_out = c_w_out[j]
        last = i == depth - 1
        res = out_proj(y, w_out.astype(BF16), xf, gain_for(i + 1), last)
        if last:
            return res.reshape(bsz, seq, d)
        xf, h = res
```

```python
import cmath
import functools
import math

import numpy as np
import jax
import jax.numpy as jnp
from jax import lax
from jax.experimental import pallas as pl
from jax.experimental.pallas import tpu as pltpu

D_MODEL = 2048
D_INNER = 4096
SEQ = 4096
CHUNK = 128
A_GROUPS = 8
A_GROUP_DIM = D_INNER // A_GROUPS
B_GROUPS = 8
B_GROUP_DIM = D_INNER // B_GROUPS
POOL_WINDOWS = (2, 4, 8, 16)
C_GROUPS = len(POOL_WINDOWS)
C_GROUP_DIM = D_INNER // C_GROUPS
EPS = 1e-6
HALO = 16

FFT_SLABS = 16
FFT_SLAB = SEQ // FFT_SLABS
BF16_ROWS = 16
LANES = 128

VMEM_LIMIT = 56 * 1024 * 1024

F32 = jnp.float32
BF16 = jnp.bfloat16


def _params(*sem):
    return pltpu.CompilerParams(dimension_semantics=sem, vmem_limit_bytes=VMEM_LIMIT)


def _norm_kernel(x_ref, g_ref, h_ref):
    x = x_ref[...]
    r = lax.rsqrt(jnp.mean(x * x, axis=-1, keepdims=True) + EPS)
    h_ref[...] = (x * r * g_ref[...]).astype(h_ref.dtype)


def rms_cast(x, g, tm=512):
    m, d = x.shape
    return pl.pallas_call(
        _norm_kernel,
        grid=(m // tm,),
        in_specs=[pl.BlockSpec((tm, d), lambda i: (i, 0)),
                  pl.BlockSpec((1, d), lambda i: (0, 0))],
        out_specs=pl.BlockSpec((tm, d), lambda i: (i, 0)),
        out_shape=jax.ShapeDtypeStruct((m, d), BF16),
        compiler_params=_params("parallel"),
        name="rms_cast",
    )(x, g.reshape(1, d))


def _act(acc, act):
    if act == "gelu":
        return jax.nn.gelu(acc)
    if act == "silu":
        return jax.nn.silu(acc)
    return acc


def _proj_kernel(a_ref, w_ref, o_ref, *refs, act, with_ssq):
    if with_ssq:
        ss_ref, wbf_ref = refs
    else:
        (wbf_ref,) = refs

    @pl.when(pl.program_id(1) == 0)
    def _():
        wbf_ref[...] = w_ref[...].astype(BF16)

    val = _act(jnp.dot(a_ref[...], wbf_ref[...], preferred_element_type=F32), act)
    o_ref[...] = val.astype(o_ref.dtype)
    if with_ssq:
        ss_ref[...] = jnp.sum(val * val, axis=-1, keepdims=True)


def proj(a, w, layer, col0, ncols, act, out_dtype, with_ssq=False, tm=1024, tn=1024):
    m, k = a.shape
    nb0 = col0 // tn
    n_tiles = ncols // tn
    out_specs = pl.BlockSpec((tm, tn), lambda j, i: (i, j))
    out_shape = jax.ShapeDtypeStruct((m, ncols), out_dtype)
    if with_ssq:
        out_specs = [out_specs, pl.BlockSpec((None, tm, 1), lambda j, i: (j, i, 0))]
        out_shape = [out_shape, jax.ShapeDtypeStruct((n_tiles, m, 1), F32)]
    return pl.pallas_call(
        functools.partial(_proj_kernel, act=act, with_ssq=with_ssq),
        grid=(n_tiles, m // tm),
        in_specs=[pl.BlockSpec((tm, k), lambda j, i: (i, 0)),
                  pl.BlockSpec((None, k, tn), lambda j, i: (layer, 0, nb0 + j))],
        out_specs=out_specs,
        out_shape=out_shape,
        scratch_shapes=[pltpu.VMEM((k, tn), BF16)],
        compiler_params=_params("arbitrary", "arbitrary"),
        name="proj_" + str(act),
    )(a, w)


def _out_kernel(y_ref, w_ref, x_ref, g_ref, *refs, n_tiles, tn, last):
    if last:
        o_ref, xrow = refs
    else:
        xo_ref, o_ref, xrow = refs
    n = pl.program_id(1)
    xn = x_ref[...] + jnp.dot(y_ref[...], w_ref[...], preferred_element_type=F32)
    if not last:
        xo_ref[...] = xn
    xrow[n] = xn

    @pl.when(n == n_tiles - 1)
    def _():
        ss = None
        for j in range(n_tiles):
            v = xrow[j]
            s = jnp.sum(v * v, axis=-1, keepdims=True)
            ss = s if ss is None else ss + s
        r = lax.rsqrt(ss * (1.0 / (n_tiles * tn)) + EPS)
        for j in range(n_tiles):
            cols = slice(j * tn, (j + 1) * tn)
            o_ref[:, cols] = (xrow[j] * r * g_ref[:, cols]).astype(o_ref.dtype)


def out_proj(y, w, x, g, last, tm=1024, tn=512):
    m, k = y.shape
    d = w.shape[1]
    n_tiles = d // tn
    in_specs = [pl.BlockSpec((tm, k), lambda i, n: (i, 0)),
                pl.BlockSpec((k, tn), lambda i, n: (0, n)),
                pl.BlockSpec((tm, tn), lambda i, n: (i, n)),
                pl.BlockSpec((1, d), lambda i, n: (0, 0))]
    row_spec = pl.BlockSpec((tm, d), lambda i, n: (i, 0))
    if last:
        out_specs = row_spec
        out_shape = jax.ShapeDtypeStruct((m, d), F32)
    else:
        out_specs = [pl.BlockSpec((tm, tn), lambda i, n: (i, n)), row_spec]
        out_shape = [jax.ShapeDtypeStruct((m, d), F32), jax.ShapeDtypeStruct((m, d), BF16)]
    return pl.pallas_call(
        functools.partial(_out_kernel, n_tiles=n_tiles, tn=tn, last=last),
        grid=(m // tm, n_tiles),
        in_specs=in_specs,
        out_specs=out_specs,
        out_shape=out_shape,
        scratch_shapes=[pltpu.VMEM((n_tiles, tm, tn), F32)],
        compiler_params=_params("parallel", "arbitrary"),
        name="out_proj",
    )(y, w, x, g.reshape(1, d))


AMIX_GROUPS_PER_STEP = 2


def _amix_kernel(h_ref, w_ref, u_ref, gv_ref, ss_ref, gain_ref, ws_ref, b_ref, o_ref, wbf_ref, *, tm):
    i, s = pl.program_id(0), pl.program_id(1)
    gps, c = AMIX_GROUPS_PER_STEP, A_GROUP_DIM

    @pl.when(i == 0)
    def _():
        for gg in range(gps):
            wbf_ref[s * gps + gg] = w_ref[:, gg * c:(gg + 1) * c].astype(BF16)

    ssq = ss_ref[0]
    for t in range(1, ss_ref.shape[0]):
        ssq = ssq + ss_ref[t]
    r = lax.rsqrt(ssq * (1.0 / D_INNER) + EPS)
    for gg in range(gps):
        cols = slice(gg * c, (gg + 1) * c)
        sg = jax.nn.silu(jnp.dot(h_ref[...], wbf_ref[s * gps + gg], preferred_element_type=F32))
        vn = (gv_ref[:, cols].astype(F32) * r * gain_ref[:, cols]).astype(BF16)
        wg = ws_ref[gg]
        bias = b_ref[gg]
        for ch in range(tm // CHUNK):
            rows = slice(ch * CHUNK, (ch + 1) * CHUNK)
            sv = jnp.dot(wg, vn[rows], preferred_element_type=F32) + bias
            y = u_ref[rows, cols].astype(F32) * sv * sg[rows]
            o_ref[rows, cols] = y.astype(o_ref.dtype)


def amix(h, w_in, layer, gate_col0, u, gv, ssq, gain, w_s, b_s, tm=1024):
    m, e = u.shape
    k = h.shape[1]
    gps = AMIX_GROUPS_PER_STEP
    c = gps * A_GROUP_DIM
    n_steps = A_GROUPS // gps
    gb0 = gate_col0 // c
    n_ss = ssq.shape[0]
    grp = pl.BlockSpec((tm, c), lambda i, s: (i, s))
    return pl.pallas_call(
        functools.partial(_amix_kernel, tm=tm),
        grid=(m // tm, n_steps),
        in_specs=[pl.BlockSpec((tm, k), lambda i, s: (i, 0)),
                  pl.BlockSpec((None, k, c),
                               lambda i, s: (layer, 0, gb0 + jnp.where(i == 0, s, n_steps - 1)),
                               pipeline_mode=pl.Buffered(1)),
                  grp, grp,
                  pl.BlockSpec((n_ss, tm, 1), lambda i, s: (0, i, 0)),
                  pl.BlockSpec((1, c), lambda i, s: (0, s)),
                  pl.BlockSpec((gps, CHUNK, CHUNK), lambda i, s: (s, 0, 0)),
                  pl.BlockSpec((gps, CHUNK, 1), lambda i, s: (s, 0, 0))],
        out_specs=grp,
        out_shape=jax.ShapeDtypeStruct((m, e), BF16),
        scratch_shapes=[pltpu.VMEM((A_GROUPS, k, A_GROUP_DIM), BF16)],
        compiler_params=_params("arbitrary", "arbitrary"),
        name="amix",
    )(h, w_in, u, gv, ssq, gain.reshape(1, e), w_s.astype(BF16), b_s.reshape(A_GROUPS, CHUNK, 1))


def _dft_tables(n):
    idx = jnp.arange(n, dtype=jnp.int32)
    ang = ((idx[:, None] * idx[None, :]) % n).astype(F32) * (2.0 * jnp.pi / n)
    return jnp.cos(ang), jnp.sin(ang)


def _fold_kernel(cs_ref, w_ref, o_ref, *, scale):
    acc = jnp.dot(cs_ref[...], w_ref[...].astype(BF16), preferred_element_type=F32)
    o_ref[...] = (acc * scale).astype(o_ref.dtype)


def fold_channel_dft(w_mix, layer):
    _, g, c, _ = w_mix.shape
    cc, sc = _dft_tables(c)
    cs = jnp.concatenate([cc, sc], axis=0).astype(BF16)
    scale = float((SEQ * c) ** -0.5)
    return pl.pallas_call(
        functools.partial(_fold_kernel, scale=scale),
        grid=(g,),
        in_specs=[pl.BlockSpec((2 * c, c), lambda i: (0, 0)),
                  pl.BlockSpec((None, None, c, c), lambda i: (layer, i, 0, 0))],
        out_specs=pl.BlockSpec((None, 2 * c, c), lambda i: (i, 0, 0)),
        out_shape=jax.ShapeDtypeStruct((g, 2 * c, c), BF16),
        compiler_params=_params("parallel"),
        name="fold_channel_dft",
    )(cs, w_mix)


def _slab_dft_tables():
    k1 = jnp.arange(FFT_SLABS, dtype=jnp.int32)[:, None, None]
    j = jnp.arange(FFT_SLAB, dtype=jnp.int32)[None, :, None]
    n2 = jnp.arange(FFT_SLAB, dtype=jnp.int32)[None, None, :]
    ang = (((k1 + FFT_SLABS * j) * n2) % SEQ).astype(F32) * (2.0 * jnp.pi / SEQ)
    cos, sin = jnp.cos(ang), jnp.sin(ang)
    s = jnp.where(k1 <= FFT_SLABS // 2, 1.0, -1.0).astype(F32)
    top = jnp.concatenate([cos, s * sin], axis=2)
    bot = jnp.concatenate([-sin, s * cos], axis=2)
    return jnp.concatenate([top, bot], axis=1).astype(BF16)


def _interleave_table():
    r = np.arange(FFT_SLAB)
    p = np.zeros((FFT_SLAB, FFT_SLAB), np.float32)
    p[r, FFT_SLABS * (r % FFT_SLABS) + r // FFT_SLABS] = 1.0
    return jnp.asarray(p, dtype=BF16)


def _root(n, e):
    return cmath.exp(-2j * math.pi * (e % n) / n)


def _snap(v):
    for t in (0.0, 1.0, -1.0):
        if abs(v - t) < 1e-12:
            return t
    return v


def _radd(x, y):
    if x is None:
        return y
    if y is None:
        return x
    return x + y


def _rsub(x, y):
    if y is None:
        return x
    if x is None:
        return -y
    return x - y


def _rscale(x, c):
    if x is None or c == 0.0:
        return None
    if c == 1.0:
        return x
    if c == -1.0:
        return -x
    return x * c


def _cmul_const(a, w):
    wr, wi = _snap(w.real), _snap(w.imag)
    return (_rsub(_rscale(a[0], wr), _rscale(a[1], wi)),
            _radd(_rscale(a[0], wi), _rscale(a[1], wr)))


def _dft16_real(xs):
    stage = [[None] * 4 for _ in range(4)]
    for m2 in range(4):
        x0, x1, x2, x3 = xs[m2], xs[4 + m2], xs[8 + m2], xs[12 + m2]
        s0, s1, d0, d1 = x0 + x2, x1 + x3, x0 - x2, x1 - x3
        stage[0][m2] = (s0 + s1, None)
        stage[2][m2] = (s0 - s1, None)
        stage[1][m2] = (d0, -d1)
        stage[3][m2] = (d0, d1)
    out = [None] * (FFT_SLABS // 2 + 1)
    for q1 in range(4):
        tw = [_cmul_const(stage[q1][m2], _root(16, m2 * q1)) for m2 in range(4)]
        for q2 in range(4):
            k1 = q1 + 4 * q2
            if k1 > FFT_SLABS // 2:
                continue
            acc = (None, None)
            for m2 in range(4):
                t = _cmul_const(tw[m2], _root(4, m2 * q2))
                acc = (_radd(acc[0], t[0]), _radd(acc[1], t[1]))
            out[k1] = acc
    return out


def _fftmix_kernel(x_ref, g_ref, wf_ref, p_ref, sg_ref, o_ref, a_ref, ym_ref, *, tc):
    half = FFT_SLABS // 2
    c = B_GROUP_DIM

    def strip(i, carry):
        r = pl.multiple_of(i * BF16_ROWS, BF16_ROWS)
        for lc in range(tc // LANES):
            lanes = slice(lc * LANES, (lc + 1) * LANES)
            xs = [x_ref[pl.ds(m * FFT_SLAB + r, BF16_ROWS), lanes].astype(F32)
                  for m in range(FFT_SLABS)]
            for k1, (re, im) in enumerate(_dft16_real(xs)):
                if im is None:
                    im = jnp.zeros_like(re)
                a_ref[k1, pl.ds(r, BF16_ROWS), lanes] = re.astype(BF16)
                a_ref[k1, pl.ds(FFT_SLAB + r, BF16_ROWS), lanes] = im.astype(BF16)
        return carry

    lax.fori_loop(0, FFT_SLAB // BF16_ROWS, strip, 0)

    for k1 in range(FFT_SLABS):
        a = a_ref[k1 if k1 <= half else FFT_SLABS - k1]
        y = jnp.dot(g_ref[k1], a, preferred_element_type=F32).astype(BF16)
        for g in range(tc // c):
            cols = slice(g * c, (g + 1) * c)
            mix = jnp.dot(y[:FFT_SLAB, cols], wf_ref[g, :c, :], preferred_element_type=F32)
            mix += jnp.dot(y[FFT_SLAB:, cols], wf_ref[g, c:, :], preferred_element_type=F32)
            mix = mix.astype(BF16)
            for k3 in range(FFT_SLABS):
                ym_ref[k3, k1 * FFT_SLABS:(k1 + 1) * FFT_SLABS, cols] = (
                    mix[k3 * FFT_SLABS:(k3 + 1) * FFT_SLABS])

    for k3 in range(FFT_SLABS):
        rows = slice(k3 * FFT_SLAB, (k3 + 1) * FFT_SLAB)
        z = jnp.dot(p_ref[...], ym_ref[k3], preferred_element_type=F32)
        o_ref[rows, :] = (z * sg_ref[rows, :].astype(F32)).astype(o_ref.dtype)


def fftmix(xb, wf, sg, bsz, tc=512):
    e = xb.shape[1]
    gpt = tc // B_GROUP_DIM
    tile = pl.BlockSpec((None, SEQ, tc), lambda b, j: (b, 0, j))
    once = pl.Buffered(1)
    out = pl.pallas_call(
        functools.partial(_fftmix_kernel, tc=tc),
        grid=(bsz, e // tc),
        in_specs=[tile,
                  pl.BlockSpec((FFT_SLABS, 2 * FFT_SLAB, 2 * FFT_SLAB), lambda b, j: (0, 0, 0),
                               pipeline_mode=once),
                  pl.BlockSpec((gpt, 2 * B_GROUP_DIM, B_GROUP_DIM), lambda b, j: (j, 0, 0)),
                  pl.BlockSpec((FFT_SLAB, FFT_SLAB), lambda b, j: (0, 0), pipeline_mode=once),
                  tile],
        out_specs=tile,
        out_shape=jax.ShapeDtypeStruct((bsz, SEQ, e), BF16),
        scratch_shapes=[pltpu.VMEM((FFT_SLABS // 2 + 1, 2 * FFT_SLAB, tc), BF16),
                        pltpu.VMEM((FFT_SLABS, FFT_SLAB, tc), BF16)],
        compiler_params=_params("parallel", "parallel"),
        name="fftmix",
    )(xb.reshape(bsz, SEQ, e), _slab_dft_tables(), wf, _interleave_table(), sg.reshape(bsz, SEQ, e))
    return out.reshape(bsz * SEQ, e)


POOL_ROWS = 256
POOL_K = 512


def _cmix_kernel(prev_ref, cur_ref, next_ref, h_ref, wg_ref, wm_ref, scale_ref, o_ref,
                 wgbf_ref, wmbf_ref, *, tm):
    g = pl.program_id(0)
    i = pl.program_id(1)

    @pl.when(i == 0)
    def _():
        wgbf_ref[...] = wg_ref[...].astype(BF16)
        wmbf_ref[...] = wm_ref[...].astype(BF16)

    win = jnp.left_shift(2, g)
    lo_off = win // 2
    hi_off = win - 1 - lo_off
    t0 = (i % (SEQ // tm)) * tm
    ext = jnp.concatenate([prev_ref[...].astype(BF16), cur_ref[...].astype(BF16),
                           next_ref[...].astype(BF16)], axis=0)
    sg = jax.nn.silu(jnp.dot(h_ref[...], wgbf_ref[...], preferred_element_type=F32))
    pooled = []
    for j in range(tm // POOL_ROWS):
        start = min(j * POOL_ROWS, tm + 2 * HALO - POOL_K)
        t = t0 + j * POOL_ROWS + lax.broadcasted_iota(jnp.int32, (POOL_ROWS, 1), 0)
        u = t0 + (start - HALO) + lax.broadcasted_iota(jnp.int32, (1, POOL_K), 1)
        lo = jnp.maximum(t - lo_off, 0)
        hi = jnp.minimum(t + hi_off, SEQ - 1)
        band = jnp.where((u >= lo) & (u <= hi), 1.0, 0.0).astype(BF16)
        wsum = jnp.dot(band, ext[start:start + POOL_K], preferred_element_type=F32)
        cnt = (hi - lo + 1).astype(F32)
        rows = slice(j * POOL_ROWS, (j + 1) * POOL_ROWS)
        pooled.append((wsum / cnt - cur_ref[rows, :]).astype(BF16))
    p = jnp.concatenate(pooled, axis=0)
    acc = jnp.dot(p, wmbf_ref[...], preferred_element_type=F32)
    o_ref[...] = (acc * scale_ref[...] * sg).astype(o_ref.dtype)


def cmix(xc, h, w_in, w_mix, layer, gate_col0, scale, tm=512):
    m, e = xc.shape
    k = h.shape[1]
    c = C_GROUP_DIM
    gb0 = gate_col0 // c
    hb = tm // HALO
    last_hb = m // HALO - 1
    return pl.pallas_call(
        functools.partial(_cmix_kernel, tm=tm),
        grid=(C_GROUPS, m // tm),
        in_specs=[pl.BlockSpec((HALO, c), lambda g, i: (jnp.maximum(i * hb - 1, 0), g)),
                  pl.BlockSpec((tm, c), lambda g, i: (i, g)),
                  pl.BlockSpec((HALO, c), lambda g, i: (jnp.minimum((i + 1) * hb, last_hb), g)),
                  pl.BlockSpec((tm, k), lambda g, i: (i, 0)),
                  pl.BlockSpec((None, k, c), lambda g, i: (layer, 0, gb0 + g)),
                  pl.BlockSpec((None, None, c, c), lambda g, i: (layer, g, 0, 0)),
                  pl.BlockSpec((1, c), lambda g, i: (0, g))],
        out_specs=pl.BlockSpec((tm, c), lambda g, i: (i, g)),
        out_shape=jax.ShapeDtypeStruct((m, e), BF16),
        scratch_shapes=[pltpu.VMEM((k, c), BF16), pltpu.VMEM((c, c), BF16)],
        compiler_params=_params("arbitrary", "arbitrary"),
        name="cmix",
    )(xc, xc, xc, h, w_in, w_mix, scale.reshape(1, e))


def kernel(x, a_norm, a_w_in, a_v_gain, a_w_s, a_b_s, a_w_out, b_norm, b_w_in, b_w_mix, b_w_out,
           c_norm, c_w_in, c_w_mix, c_scale, c_w_out, final_norm):
    bsz, seq, d = x.shape
    assert (seq, d) == (SEQ, D_MODEL)
    e = D_INNER
    depth = 4
    norms = {0: a_norm, 1: b_norm, 2: c_norm}

    def gain_for(i):
        if i == depth:
            return final_norm
        return norms[i % 3][i // 3]

    xf = x.reshape(bsz * seq, d)
    h = rms_cast(xf, gain_for(0))
    for i in range(depth):
        kind, j = i % 3, i // 3
        if kind == 0:
            u = proj(h, a_w_in, j, 0, e, "gelu", BF16)
            gv, ssq = proj(h, a_w_in, j, e, e, "gelu", BF16, with_ssq=True)
            y = amix(h, a_w_in, j, 2 * e, u, gv, ssq, a_v_gain[j], a_w_s[j], a_b_s[j])
            w_out = a_w_out[j]
        elif kind == 1:
            xb = proj(h, b_w_in, j, 0, e, None, BF16)
            sg = proj(h, b_w_in, j, e, e, "silu", BF16)
            wf = fold_channel_dft(b_w_mix, j)
            y = fftmix(xb, wf, sg, bsz)
            w_out = b_w_out[j]
        else:
            xc = proj(h, c_w_in, j, 0, e, None, F32)
            y = cmix(xc, h, c_w_in, c_w_mix, j, e, c_scale[j])
            w_out = c_w_out[j]
        last = i == depth - 1
        res = out_proj(y, w_out.astype(BF16), xf, gain_for(i + 1), last)
        if last:
            return res.reshape(bsz, seq, d)
        xf, h = res
```

```python
import functools

import numpy as np
import jax
import jax.numpy as jnp
from jax import lax
from jax.experimental import pallas as pl
from jax.experimental.pallas import tpu as pltpu

D_MODEL = 2048
D_INNER = 4096
SEQ = 4096
CHUNK = 128
A_GROUPS = 8
A_GROUP_DIM = D_INNER // A_GROUPS
B_GROUPS = 8
B_GROUP_DIM = D_INNER // B_GROUPS
POOL_WINDOWS = (2, 4, 8, 16)
C_GROUPS = len(POOL_WINDOWS)
C_GROUP_DIM = D_INNER // C_GROUPS
EPS = 1e-6
HALO = 16

FFT_SLABS = 16
FFT_SLAB = SEQ // FFT_SLABS
BF16_ROWS = 16

VMEM_LIMIT = 56 * 1024 * 1024

F32 = jnp.float32
BF16 = jnp.bfloat16


def _params(*sem):
    return pltpu.CompilerParams(dimension_semantics=sem, vmem_limit_bytes=VMEM_LIMIT)


def _norm_kernel(x_ref, g_ref, h_ref):
    x = x_ref[...]
    r = lax.rsqrt(jnp.mean(x * x, axis=-1, keepdims=True) + EPS)
    h_ref[...] = (x * r * g_ref[...]).astype(h_ref.dtype)


def rms_cast(x, g, tm=512):
    m, d = x.shape
    return pl.pallas_call(
        _norm_kernel,
        grid=(m // tm,),
        in_specs=[pl.BlockSpec((tm, d), lambda i: (i, 0)),
                  pl.BlockSpec((1, d), lambda i: (0, 0))],
        out_specs=pl.BlockSpec((tm, d), lambda i: (i, 0)),
        out_shape=jax.ShapeDtypeStruct((m, d), BF16),
        compiler_params=_params("parallel"),
        name="rms_cast",
    )(x, g.reshape(1, d))


GELU_C = 0.7978845608028654
GELU_A = 0.044715


def _act(x, act):
    if act is None:
        return x
    hx = 0.5 * x
    if act == "gelu":
        t = jnp.tanh(x * (GELU_C + (GELU_C * GELU_A) * (x * x)))
    else:
        t = jnp.tanh(hx)
    return hx + hx * t


def _proj_kernel(a_ref, w_ref, o_ref, *refs, act, with_ssq):
    if with_ssq:
        ss_ref, wbf_ref = refs
    else:
        (wbf_ref,) = refs

    @pl.when(pl.program_id(1) == 0)
    def _():
        wbf_ref[...] = w_ref[...].astype(BF16)

    val = _act(jnp.dot(a_ref[...], wbf_ref[...], preferred_element_type=F32), act)
    o_ref[...] = val.astype(o_ref.dtype)
    if with_ssq:
        ss_ref[...] = jnp.sum(val * val, axis=-1, keepdims=True)


def proj(a, w, layer, col0, ncols, act, out_dtype, with_ssq=False, tm=1024, tn=1024):
    m, k = a.shape
    nb0 = col0 // tn
    n_tiles = ncols // tn
    out_specs = pl.BlockSpec((tm, tn), lambda j, i: (i, j))
    out_shape = jax.ShapeDtypeStruct((m, ncols), out_dtype)
    if with_ssq:
        out_specs = [out_specs, pl.BlockSpec((None, tm, 1), lambda j, i: (j, i, 0))]
        out_shape = [out_shape, jax.ShapeDtypeStruct((n_tiles, m, 1), F32)]
    return pl.pallas_call(
        functools.partial(_proj_kernel, act=act, with_ssq=with_ssq),
        grid=(n_tiles, m // tm),
        in_specs=[pl.BlockSpec((tm, k), lambda j, i: (i, 0)),
                  pl.BlockSpec((None, k, tn), lambda j, i: (layer, 0, nb0 + j))],
        out_specs=out_specs,
        out_shape=out_shape,
        scratch_shapes=[pltpu.VMEM((k, tn), BF16)],
        compiler_params=_params("arbitrary", "arbitrary"),
        name="proj_" + str(act),
    )(a, w)


def _out_kernel(y_ref, w_ref, x_ref, g_ref, *refs, n_tiles, tn, last):
    if last:
        o_ref, xrow = refs
    else:
        xo_ref, o_ref, xrow = refs
    n = pl.program_id(1)
    xn = x_ref[...] + jnp.dot(y_ref[...], w_ref[...], preferred_element_type=F32)
    if not last:
        xo_ref[...] = xn
    xrow[n] = xn

    @pl.when(n == n_tiles - 1)
    def _():
        ss = None
        for j in range(n_tiles):
            v = xrow[j]
            s = jnp.sum(v * v, axis=-1, keepdims=True)
            ss = s if ss is None else ss + s
        r = lax.rsqrt(ss * (1.0 / (n_tiles * tn)) + EPS)
        for j in range(n_tiles):
            cols = slice(j * tn, (j + 1) * tn)
            o_ref[:, cols] = (xrow[j] * r * g_ref[:, cols]).astype(o_ref.dtype)


def out_proj(y, w, layer, x, g, last, tm=1024, tn=512):
    m, k = y.shape
    d = w.shape[2]
    n_tiles = d // tn
    in_specs = [pl.BlockSpec((tm, k), lambda i, n: (i, 0)),
                pl.BlockSpec((None, k, tn), lambda i, n: (layer, 0, n)),
                pl.BlockSpec((tm, tn), lambda i, n: (i, n)),
                pl.BlockSpec((1, d), lambda i, n: (0, 0))]
    row_spec = pl.BlockSpec((tm, d), lambda i, n: (i, 0))
    if last:
        out_specs = row_spec
        out_shape = jax.ShapeDtypeStruct((m, d), F32)
    else:
        out_specs = [pl.BlockSpec((tm, tn), lambda i, n: (i, n)), row_spec]
        out_shape = [jax.ShapeDtypeStruct((m, d), F32), jax.ShapeDtypeStruct((m, d), BF16)]
    return pl.pallas_call(
        functools.partial(_out_kernel, n_tiles=n_tiles, tn=tn, last=last),
        grid=(m // tm, n_tiles),
        in_specs=in_specs,
        out_specs=out_specs,
        out_shape=out_shape,
        scratch_shapes=[pltpu.VMEM((n_tiles, tm, tn), F32)],
        compiler_params=_params("parallel", "arbitrary"),
        name="out_proj",
    )(y, w, x, g.reshape(1, d))


AMIX_GROUPS_PER_STEP = 2


def _amix_kernel(h_ref, w_ref, u_ref, gv_ref, ss_ref, gain_ref, ws_ref, b_ref, o_ref, wbf_ref, *, tm):
    i, s = pl.program_id(0), pl.program_id(1)
    gps, c = AMIX_GROUPS_PER_STEP, A_GROUP_DIM

    @pl.when(i == 0)
    def _():
        for gg in range(gps):
            wbf_ref[s * gps + gg] = w_ref[:, gg * c:(gg + 1) * c].astype(BF16)

    ssq = ss_ref[0]
    for t in range(1, ss_ref.shape[0]):
        ssq = ssq + ss_ref[t]
    r = lax.rsqrt(ssq * (1.0 / D_INNER) + EPS)
    for gg in range(gps):
        cols = slice(gg * c, (gg + 1) * c)
        sg = _act(jnp.dot(h_ref[...], wbf_ref[s * gps + gg], preferred_element_type=F32), "silu")
        vn = (gv_ref[:, cols].astype(F32) * r * gain_ref[:, cols]).astype(BF16)
        wg = ws_ref[gg]
        bias = b_ref[gg]
        for ch in range(tm // CHUNK):
            rows = slice(ch * CHUNK, (ch + 1) * CHUNK)
            sv = jnp.dot(wg, vn[rows], preferred_element_type=F32) + bias
            y = u_ref[rows, cols].astype(F32) * sv * sg[rows]
            o_ref[rows, cols] = y.astype(o_ref.dtype)


def amix(h, w_in, layer, gate_col0, u, gv, ssq, gain, w_s, b_s, tm=1024):
    m, e = u.shape
    k = h.shape[1]
    gps = AMIX_GROUPS_PER_STEP
    c = gps * A_GROUP_DIM
    n_steps = A_GROUPS // gps
    gb0 = gate_col0 // c
    n_ss = ssq.shape[0]
    grp = pl.BlockSpec((tm, c), lambda i, s: (i, s))
    return pl.pallas_call(
        functools.partial(_amix_kernel, tm=tm),
        grid=(m // tm, n_steps),
        in_specs=[pl.BlockSpec((tm, k), lambda i, s: (i, 0)),
                  pl.BlockSpec((None, k, c),
                               lambda i, s: (layer, 0, gb0 + jnp.where(i == 0, s, n_steps - 1)),
                               pipeline_mode=pl.Buffered(1)),
                  grp, grp,
                  pl.BlockSpec((n_ss, tm, 1), lambda i, s: (0, i, 0)),
                  pl.BlockSpec((1, c), lambda i, s: (0, s)),
                  pl.BlockSpec((gps, CHUNK, CHUNK), lambda i, s: (s, 0, 0)),
                  pl.BlockSpec((gps, CHUNK, 1), lambda i, s: (s, 0, 0))],
        out_specs=grp,
        out_shape=jax.ShapeDtypeStruct((m, e), BF16),
        scratch_shapes=[pltpu.VMEM((A_GROUPS, k, A_GROUP_DIM), BF16)],
        compiler_params=_params("arbitrary", "arbitrary"),
        name="amix",
    )(h, w_in, u, gv, ssq, gain.reshape(1, e), w_s.astype(BF16), b_s.reshape(A_GROUPS, CHUNK, 1))


def _dft_tables(n):
    idx = jnp.arange(n, dtype=jnp.int32)
    ang = ((idx[:, None] * idx[None, :]) % n).astype(F32) * (2.0 * jnp.pi / n)
    return jnp.cos(ang), jnp.sin(ang)


def _fold_kernel(cs_ref, w_ref, o_ref, *, scale):
    acc = jnp.dot(cs_ref[...], w_ref[...].astype(BF16), preferred_element_type=F32)
    o_ref[...] = (acc * scale).astype(o_ref.dtype)


def fold_channel_dft(w_mix, layer):
    _, g, c, _ = w_mix.shape
    cc, sc = _dft_tables(c)
    cs = jnp.concatenate([cc, sc], axis=0).astype(BF16)
    scale = float((SEQ * c) ** -0.5)
    return pl.pallas_call(
        functools.partial(_fold_kernel, scale=scale),
        grid=(g,),
        in_specs=[pl.BlockSpec((2 * c, c), lambda i: (0, 0)),
                  pl.BlockSpec((None, None, c, c), lambda i: (layer, i, 0, 0))],
        out_specs=pl.BlockSpec((None, 2 * c, c), lambda i: (i, 0, 0)),
        out_shape=jax.ShapeDtypeStruct((g, 2 * c, c), BF16),
        compiler_params=_params("parallel"),
        name="fold_channel_dft",
    )(cs, w_mix)


def _slab_dft_tables():
    k1 = jnp.arange(FFT_SLABS, dtype=jnp.int32)[:, None, None]
    j = jnp.arange(FFT_SLAB, dtype=jnp.int32)[None, :, None]
    n2 = jnp.arange(FFT_SLAB, dtype=jnp.int32)[None, None, :]
    ang = (((k1 + FFT_SLABS * j) * n2) % SEQ).astype(F32) * (2.0 * jnp.pi / SEQ)
    cos, sin = jnp.cos(ang), jnp.sin(ang)
    s = jnp.where(k1 <= FFT_SLABS // 2, 1.0, -1.0).astype(F32)
    top = jnp.concatenate([cos, s * sin], axis=2)
    bot = jnp.concatenate([-sin, s * cos], axis=2)
    return jnp.concatenate([top, bot], axis=1).astype(BF16)


def _interleave_table():
    r = np.arange(FFT_SLAB)
    p = np.zeros((FFT_SLAB, FFT_SLAB), np.float32)
    p[r, FFT_SLABS * (r % FFT_SLABS) + r // FFT_SLABS] = 1.0
    return jnp.asarray(p, dtype=BF16)


_STAGE1_BLOCKS = ([(0, 0)] + [(k1, part) for k1 in range(1, FFT_SLABS // 2) for part in (0, 1)]
                  + [(FFT_SLABS // 2, 0)])


def _stage1_table():
    k1 = jnp.asarray([b[0] for b in _STAGE1_BLOCKS], jnp.int32)[:, None]
    is_im = jnp.asarray([b[1] for b in _STAGE1_BLOCKS], jnp.int32)[:, None]
    m = jnp.arange(FFT_SLABS, dtype=jnp.int32)[None, :]
    ang = ((k1 * m) % FFT_SLABS).astype(F32) * (2.0 * jnp.pi / FFT_SLABS)
    coef = jnp.where(is_im == 1, -jnp.sin(ang), jnp.cos(ang))
    return jnp.kron(coef, jnp.eye(BF16_ROWS, dtype=F32)).astype(BF16)


def _fftmix_kernel(x_ref, m1_ref, g_ref, wf_ref, p_ref, sg_ref, o_ref, a_ref, ym_ref, *, tc):
    half = FFT_SLABS // 2
    c = B_GROUP_DIM

    for grp in range(FFT_SLAB // BF16_ROWS):
        r0 = grp * BF16_ROWS
        xg = jnp.concatenate([x_ref[m * FFT_SLAB + r0:m * FFT_SLAB + r0 + BF16_ROWS, :]
                              for m in range(FFT_SLABS)], axis=0)
        s1 = jnp.dot(m1_ref[...], xg, preferred_element_type=F32).astype(BF16)
        for blk, (k1, part) in enumerate(_STAGE1_BLOCKS):
            a_ref[k1, part * FFT_SLAB + r0:part * FFT_SLAB + r0 + BF16_ROWS, :] = (
                s1[blk * BF16_ROWS:(blk + 1) * BF16_ROWS])

    for k1 in range(FFT_SLABS):
        if k1 % half == 0:
            y = jnp.dot(g_ref[k1, :, :FFT_SLAB], a_ref[k1, :FFT_SLAB, :], preferred_element_type=F32)
        else:
            a = a_ref[k1 if k1 < half else FFT_SLABS - k1]
            y = jnp.dot(g_ref[k1], a, preferred_element_type=F32)
        y = y.astype(BF16)
        for g in range(tc // c):
            cols = slice(g * c, (g + 1) * c)
            mix = jnp.dot(y[:FFT_SLAB, cols], wf_ref[g, :c, :], preferred_element_type=F32)
            mix += jnp.dot(y[FFT_SLAB:, cols], wf_ref[g, c:, :], preferred_element_type=F32)
            mix = mix.astype(BF16)
            for k3 in range(FFT_SLABS):
                ym_ref[k3, k1 * FFT_SLABS:(k1 + 1) * FFT_SLABS, cols] = (
                    mix[k3 * FFT_SLABS:(k3 + 1) * FFT_SLABS])

    for k3 in range(FFT_SLABS):
        rows = slice(k3 * FFT_SLAB, (k3 + 1) * FFT_SLAB)
        z = jnp.dot(p_ref[...], ym_ref[k3], preferred_element_type=F32)
        o_ref[rows, :] = (z * sg_ref[rows, :].astype(F32)).astype(o_ref.dtype)


def fftmix(xb, wf, sg, bsz, tc=512):
    e = xb.shape[1]
    gpt = tc // B_GROUP_DIM
    tile = pl.BlockSpec((None, SEQ, tc), lambda b, j: (b, 0, j))
    once = pl.Buffered(1)
    out = pl.pallas_call(
        functools.partial(_fftmix_kernel, tc=tc),
        grid=(bsz, e // tc),
        in_specs=[tile,
                  pl.BlockSpec((FFT_SLAB, FFT_SLAB), lambda b, j: (0, 0), pipeline_mode=once),
                  pl.BlockSpec((FFT_SLABS, 2 * FFT_SLAB, 2 * FFT_SLAB), lambda b, j: (0, 0, 0),
                               pipeline_mode=once),
                  pl.BlockSpec((gpt, 2 * B_GROUP_DIM, B_GROUP_DIM), lambda b, j: (j, 0, 0)),
                  pl.BlockSpec((FFT_SLAB, FFT_SLAB), lambda b, j: (0, 0), pipeline_mode=once),
                  tile],
        out_specs=tile,
        out_shape=jax.ShapeDtypeStruct((bsz, SEQ, e), BF16),
        scratch_shapes=[pltpu.VMEM((FFT_SLABS // 2 + 1, 2 * FFT_SLAB, tc), BF16),
                        pltpu.VMEM((FFT_SLABS, FFT_SLAB, tc), BF16)],
        compiler_params=_params("parallel", "parallel"),
        name="fftmix",
    )(xb.reshape(bsz, SEQ, e), _stage1_table(), _slab_dft_tables(), wf, _interleave_table(),
      sg.reshape(bsz, SEQ, e))
    return out.reshape(bsz * SEQ, e)


POOL_ROWS = 256
POOL_K = 512


def _cmix_kernel(prev_ref, cur_ref, next_ref, h_ref, wg_ref, wm_ref, scale_ref, o_ref,
                 wgbf_ref, wmbf_ref, *, tm):
    g = pl.program_id(0)
    i = pl.program_id(1)

    @pl.when(i == 0)
    def _():
        wgbf_ref[...] = wg_ref[...].astype(BF16)
        wmbf_ref[...] = wm_ref[...].astype(BF16)

    win = jnp.left_shift(2, g)
    lo_off = win // 2
    hi_off = win - 1 - lo_off
    t0 = (i % (SEQ // tm)) * tm
    ext = jnp.concatenate([prev_ref[...].astype(BF16), cur_ref[...].astype(BF16),
                           next_ref[...].astype(BF16)], axis=0)
    sg = _act(jnp.dot(h_ref[...], wgbf_ref[...], preferred_element_type=F32), "silu")
    pooled = []
    for j in range(tm // POOL_ROWS):
        start = min(j * POOL_ROWS, tm + 2 * HALO - POOL_K)
        t = t0 + j * POOL_ROWS + lax.broadcasted_iota(jnp.int32, (POOL_ROWS, 1), 0)
        u = t0 + (start - HALO) + lax.broadcasted_iota(jnp.int32, (1, POOL_K), 1)
        lo = jnp.maximum(t - lo_off, 0)
        hi = jnp.minimum(t + hi_off, SEQ - 1)
        band = jnp.where((u >= lo) & (u <= hi), 1.0, 0.0).astype(BF16)
        wsum = jnp.dot(band, ext[start:start + POOL_K], preferred_element_type=F32)
        cnt = (hi - lo + 1).astype(F32)
        rows = slice(j * POOL_ROWS, (j + 1) * POOL_ROWS)
        pooled.append((wsum / cnt - cur_ref[rows, :]).astype(BF16))
    p = jnp.concatenate(pooled, axis=0)
    acc = jnp.dot(p, wmbf_ref[...], preferred_element_type=F32)
    o_ref[...] = (acc * scale_ref[...] * sg).astype(o_ref.dtype)


def cmix(xc, h, w_in, w_mix, layer, gate_col0, scale, tm=512):
    m, e = xc.shape
    k = h.shape[1]
    c = C_GROUP_DIM
    gb0 = gate_col0 // c
    hb = tm // HALO
    last_hb = m // HALO - 1
    return pl.pallas_call(
        functools.partial(_cmix_kernel, tm=tm),
        grid=(C_GROUPS, m // tm),
        in_specs=[pl.BlockSpec((HALO, c), lambda g, i: (jnp.maximum(i * hb - 1, 0), g)),
                  pl.BlockSpec((tm, c), lambda g, i: (i, g)),
                  pl.BlockSpec((HALO, c), lambda g, i: (jnp.minimum((i + 1) * hb, last_hb), g)),
                  pl.BlockSpec((tm, k), lambda g, i: (i, 0)),
                  pl.BlockSpec((None, k, c), lambda g, i: (layer, 0, gb0 + g)),
                  pl.BlockSpec((None, None, c, c), lambda g, i: (layer, g, 0, 0)),
                  pl.BlockSpec((1, c), lambda g, i: (0, g))],
        out_specs=pl.BlockSpec((tm, c), lambda g, i: (i, g)),
        out_shape=jax.ShapeDtypeStruct((m, e), BF16),
        scratch_shapes=[pltpu.VMEM((k, c), BF16), pltpu.VMEM((c, c), BF16)],
        compiler_params=_params("arbitrary", "arbitrary"),
        name="cmix",
    )(xc, xc, xc, h, w_in, w_mix, scale.reshape(1, e))


def kernel(x, a_norm, a_w_in, a_v_gain, a_w_s, a_b_s, a_w_out, b_norm, b_w_in, b_w_mix, b_w_out,
           c_norm, c_w_in, c_w_mix, c_scale, c_w_out, final_norm):
    bsz, seq, d = x.shape
    assert (seq, d) == (SEQ, D_MODEL)
    e = D_INNER
    depth = 4
    norms = {0: a_norm, 1: b_norm, 2: c_norm}

    def gain_for(i):
        if i == depth:
            return final_norm
        return norms[i % 3][i // 3]

    w_outs = {0: a_w_out.astype(BF16), 1: b_w_out.astype(BF16), 2: c_w_out.astype(BF16)}
    xf = x.reshape(bsz * seq, d)
    h = rms_cast(xf, gain_for(0))
    for i in range(depth):
        kind, j = i % 3, i // 3
        if kind == 0:
            u = proj(h, a_w_in, j, 0, e, "gelu", BF16)
            gv, ssq = proj(h, a_w_in, j, e, e, "gelu", BF16, with_ssq=True)
            y = amix(h, a_w_in, j, 2 * e, u, gv, ssq, a_v_gain[j], a_w_s[j], a_b_s[j])
        elif kind == 1:
            xb = proj(h, b_w_in, j, 0, e, None, BF16)
            sg = proj(h, b_w_in, j, e, e, "silu", BF16)
            wf = fold_channel_dft(b_w_mix, j)
            y = fftmix(xb, wf, sg, bsz)
        else:
            xc = proj(h, c_w_in, j, 0, e, None, F32)
            y = cmix(xc, h, c_w_in, c_w_mix, j, e, c_scale[j])
        last = i == depth - 1
        res = out_proj(y, w_outs[kind], j, xf, gain_for(i + 1), last)
        if last:
            return res.reshape(bsz, seq, d)
        xf, h = res
```

```python
import functools

import numpy as np
import jax
import jax.numpy as jnp
from jax import lax
from jax.experimental import pallas as pl
from jax.experimental.pallas import tpu as pltpu

D_MODEL = 2048
D_INNER = 4096
SEQ = 4096
CHUNK = 128
A_GROUPS = 8
A_GROUP_DIM = D_INNER // A_GROUPS
B_GROUPS = 8
B_GROUP_DIM = D_INNER // B_GROUPS
POOL_WINDOWS = (2, 4, 8, 16)
C_GROUPS = len(POOL_WINDOWS)
C_GROUP_DIM = D_INNER // C_GROUPS
EPS = 1e-6
HALO = 16

FFT_SLABS = 16
FFT_SLAB = SEQ // FFT_SLABS
BF16_ROWS = 16

VMEM_LIMIT = 56 * 1024 * 1024

F32 = jnp.float32
BF16 = jnp.bfloat16


def _params(*sem):
    return pltpu.CompilerParams(dimension_semantics=sem, vmem_limit_bytes=VMEM_LIMIT)


def _norm_kernel(x_ref, g_ref, h_ref):
    x = x_ref[...]
    r = lax.rsqrt(jnp.mean(x * x, axis=-1, keepdims=True) + EPS)
    h_ref[...] = (x * r * g_ref[...]).astype(h_ref.dtype)


def rms_cast(x, g, tm=512):
    m, d = x.shape
    return pl.pallas_call(
        _norm_kernel,
        grid=(m // tm,),
        in_specs=[pl.BlockSpec((tm, d), lambda i: (i, 0)),
                  pl.BlockSpec((1, d), lambda i: (0, 0))],
        out_specs=pl.BlockSpec((tm, d), lambda i: (i, 0)),
        out_shape=jax.ShapeDtypeStruct((m, d), BF16),
        compiler_params=_params("parallel"),
        name="rms_cast",
    )(x, g.reshape(1, d))


GELU_C = 0.7978845608028654
GELU_A = 0.044715


def _act(x, act):
    if act is None:
        return x
    hx = 0.5 * x
    if act == "gelu":
        t = jnp.tanh(x * (GELU_C + (GELU_C * GELU_A) * (x * x)))
    else:
        t = jnp.tanh(hx)
    return hx + hx * t


def _proj_kernel(a_ref, w_ref, *refs, act, with_ssq, with_cast):
    refs = list(refs)
    cast_in = refs.pop(0) if with_cast else None
    o_ref = refs.pop(0)
    ss_ref = refs.pop(0) if with_ssq else None
    cast_out = refs.pop(0) if with_cast else None
    (wbf_ref,) = refs

    @pl.when(pl.program_id(1) == 0)
    def _():
        wbf_ref[...] = w_ref[...].astype(BF16)

    val = _act(jnp.dot(a_ref[...], wbf_ref[...], preferred_element_type=F32), act)
    o_ref[...] = val.astype(o_ref.dtype)
    if with_ssq:
        ss_ref[...] = jnp.sum(val * val, axis=-1, keepdims=True)
    if with_cast:
        cast_out[...] = cast_in[...].astype(cast_out.dtype)


def proj(a, w, layer, col0, ncols, act, out_dtype, with_ssq=False, cast=None, tm=1024, tn=1024):
    m, k = a.shape
    nb0 = col0 // tn
    n_tiles = ncols // tn
    n_m = m // tm
    in_specs = [pl.BlockSpec((tm, k), lambda j, i: (i, 0)),
                pl.BlockSpec((None, k, tn), lambda j, i: (layer, 0, nb0 + j))]
    args = [a, w]
    out_specs = [pl.BlockSpec((tm, tn), lambda j, i: (i, j))]
    out_shape = [jax.ShapeDtypeStruct((m, ncols), out_dtype)]
    if with_ssq:
        out_specs.append(pl.BlockSpec((None, tm, 1), lambda j, i: (j, i, 0)))
        out_shape.append(jax.ShapeDtypeStruct((n_tiles, m, 1), F32))
    if cast is not None:
        cw, cl = cast
        _, ce, cd = cw.shape
        slab = ce // (n_tiles * n_m)
        assert slab * n_tiles * n_m == ce and slab % BF16_ROWS == 0
        in_specs.append(pl.BlockSpec((None, slab, cd), lambda j, i: (cl, j * n_m + i, 0)))
        args.append(cw)
        out_specs.append(pl.BlockSpec((slab, cd), lambda j, i: (j * n_m + i, 0)))
        out_shape.append(jax.ShapeDtypeStruct((ce, cd), BF16))
    return pl.pallas_call(
        functools.partial(_proj_kernel, act=act, with_ssq=with_ssq, with_cast=cast is not None),
        grid=(n_tiles, n_m),
        in_specs=in_specs,
        out_specs=out_specs,
        out_shape=out_shape,
        scratch_shapes=[pltpu.VMEM((k, tn), BF16)],
        compiler_params=_params("arbitrary", "arbitrary"),
        name="proj_" + str(act),
    )(*args)


def _out_kernel(y_ref, w_ref, x_ref, g_ref, *refs, n_tiles, tn, last):
    if last:
        o_ref, xrow = refs
    else:
        xo_ref, o_ref, xrow = refs
    n = pl.program_id(1)
    xn = x_ref[...] + jnp.dot(y_ref[...], w_ref[...], preferred_element_type=F32)
    if not last:
        xo_ref[...] = xn
    xrow[n] = xn

    @pl.when(n == n_tiles - 1)
    def _():
        ss = None
        for j in range(n_tiles):
            v = xrow[j]
            s = jnp.sum(v * v, axis=-1, keepdims=True)
            ss = s if ss is None else ss + s
        r = lax.rsqrt(ss * (1.0 / (n_tiles * tn)) + EPS)
        for j in range(n_tiles):
            cols = slice(j * tn, (j + 1) * tn)
            o_ref[:, cols] = (xrow[j] * r * g_ref[:, cols]).astype(o_ref.dtype)


def out_proj(y, w, x, g, last, tm=1024, tn=512):
    m, k = y.shape
    d = w.shape[1]
    n_tiles = d // tn
    in_specs = [pl.BlockSpec((tm, k), lambda i, n: (i, 0)),
                pl.BlockSpec((k, tn), lambda i, n: (0, n)),
                pl.BlockSpec((tm, tn), lambda i, n: (i, n)),
                pl.BlockSpec((1, d), lambda i, n: (0, 0))]
    row_spec = pl.BlockSpec((tm, d), lambda i, n: (i, 0))
    if last:
        out_specs = row_spec
        out_shape = jax.ShapeDtypeStruct((m, d), F32)
    else:
        out_specs = [pl.BlockSpec((tm, tn), lambda i, n: (i, n)), row_spec]
        out_shape = [jax.ShapeDtypeStruct((m, d), F32), jax.ShapeDtypeStruct((m, d), BF16)]
    return pl.pallas_call(
        functools.partial(_out_kernel, n_tiles=n_tiles, tn=tn, last=last),
        grid=(m // tm, n_tiles),
        in_specs=in_specs,
        out_specs=out_specs,
        out_shape=out_shape,
        scratch_shapes=[pltpu.VMEM((n_tiles, tm, tn), F32)],
        compiler_params=_params("parallel", "arbitrary"),
        name="out_proj",
    )(y, w, x, g.reshape(1, d))


AMIX_GROUPS_PER_STEP = 2


def _amix_kernel(h_ref, w_ref, u_ref, gv_ref, ss_ref, gain_ref, ws_ref, b_ref, o_ref, wbf_ref, t_ref,
                 *, tm):
    i, s = pl.program_id(0), pl.program_id(1)
    gps, c = AMIX_GROUPS_PER_STEP, A_GROUP_DIM

    @pl.when(i == 0)
    def _():
        for gg in range(gps):
            wbf_ref[s * gps + gg] = w_ref[:, gg * c:(gg + 1) * c].astype(BF16)

    ssq = ss_ref[0]
    for t in range(1, ss_ref.shape[0]):
        ssq = ssq + ss_ref[t]
    r = lax.rsqrt(ssq * (1.0 / D_INNER) + EPS)
    for gg in range(gps):
        cols = slice(gg * c, (gg + 1) * c)
        vn = (gv_ref[:, cols].astype(F32) * r * gain_ref[:, cols]).astype(BF16)
        wg = ws_ref[gg]
        bias = b_ref[gg]
        for ch in range(tm // CHUNK):
            rows = slice(ch * CHUNK, (ch + 1) * CHUNK)
            sv = jnp.dot(wg, vn[rows], preferred_element_type=F32) + bias
            t_ref[rows, cols] = u_ref[rows, cols].astype(F32) * sv
        sg = _act(jnp.dot(h_ref[...], wbf_ref[s * gps + gg], preferred_element_type=F32), "silu")
        o_ref[:, cols] = (t_ref[:, cols] * sg).astype(o_ref.dtype)


def amix(h, w_in, layer, gate_col0, u, gv, ssq, gain, w_s, b_s, tm=1024):
    m, e = u.shape
    k = h.shape[1]
    gps = AMIX_GROUPS_PER_STEP
    c = gps * A_GROUP_DIM
    n_steps = A_GROUPS // gps
    gb0 = gate_col0 // c
    n_ss = ssq.shape[0]
    grp = pl.BlockSpec((tm, c), lambda i, s: (i, s))
    return pl.pallas_call(
        functools.partial(_amix_kernel, tm=tm),
        grid=(m // tm, n_steps),
        in_specs=[pl.BlockSpec((tm, k), lambda i, s: (i, 0)),
                  pl.BlockSpec((None, k, c),
                               lambda i, s: (layer, 0, gb0 + jnp.where(i == 0, s, n_steps - 1)),
                               pipeline_mode=pl.Buffered(1)),
                  grp, grp,
                  pl.BlockSpec((n_ss, tm, 1), lambda i, s: (0, i, 0)),
                  pl.BlockSpec((1, c), lambda i, s: (0, s)),
                  pl.BlockSpec((gps, CHUNK, CHUNK), lambda i, s: (s, 0, 0)),
                  pl.BlockSpec((gps, CHUNK, 1), lambda i, s: (s, 0, 0))],
        out_specs=grp,
        out_shape=jax.ShapeDtypeStruct((m, e), BF16),
        scratch_shapes=[pltpu.VMEM((A_GROUPS, k, A_GROUP_DIM), BF16), pltpu.VMEM((tm, c), F32)],
        compiler_params=_params("arbitrary", "arbitrary"),
        name="amix",
    )(h, w_in, u, gv, ssq, gain.reshape(1, e), w_s.astype(BF16), b_s.reshape(A_GROUPS, CHUNK, 1))


def _dft_tables(n):
    idx = jnp.arange(n, dtype=jnp.int32)
    ang = ((idx[:, None] * idx[None, :]) % n).astype(F32) * (2.0 * jnp.pi / n)
    return jnp.cos(ang), jnp.sin(ang)


def _fold_kernel(cs_ref, w_ref, o_ref, *, scale):
    acc = jnp.dot(cs_ref[...], w_ref[...].astype(BF16), preferred_element_type=F32)
    o_ref[...] = (acc * scale).astype(o_ref.dtype)


def fold_channel_dft(w_mix, layer):
    _, g, c, _ = w_mix.shape
    cc, sc = _dft_tables(c)
    cs = jnp.concatenate([cc, sc], axis=0).astype(BF16)
    scale = float((SEQ * c) ** -0.5)
    return pl.pallas_call(
        functools.partial(_fold_kernel, scale=scale),
        grid=(g,),
        in_specs=[pl.BlockSpec((2 * c, c), lambda i: (0, 0)),
                  pl.BlockSpec((None, None, c, c), lambda i: (layer, i, 0, 0))],
        out_specs=pl.BlockSpec((None, 2 * c, c), lambda i: (i, 0, 0)),
        out_shape=jax.ShapeDtypeStruct((g, 2 * c, c), BF16),
        compiler_params=_params("parallel"),
        name="fold_channel_dft",
    )(cs, w_mix)


def _slab_dft_tables():
    k1 = jnp.arange(FFT_SLABS, dtype=jnp.int32)[:, None, None]
    j = jnp.arange(FFT_SLAB, dtype=jnp.int32)[None, :, None]
    n2 = jnp.arange(FFT_SLAB, dtype=jnp.int32)[None, None, :]
    a1 = ((k1 * n2) % SEQ).astype(F32) * (2.0 * jnp.pi / SEQ)
    a2 = ((j * n2) % FFT_SLAB).astype(F32) * (2.0 * jnp.pi / FFT_SLAB)
    c1, s1, c2, s2 = jnp.cos(a1), jnp.sin(a1), jnp.cos(a2), jnp.sin(a2)
    cos, sin = c1 * c2 - s1 * s2, s1 * c2 + c1 * s2
    s = jnp.where(k1 <= FFT_SLABS // 2, 1.0, -1.0).astype(F32)
    top = jnp.concatenate([cos, s * sin], axis=2)
    bot = jnp.concatenate([-sin, s * cos], axis=2)
    return jnp.concatenate([top, bot], axis=1).astype(BF16)


def _interleave_table():
    r = np.arange(FFT_SLAB)
    p = np.zeros((FFT_SLAB, FFT_SLAB), np.float32)
    p[r, FFT_SLABS * (r % FFT_SLABS) + r // FFT_SLABS] = 1.0
    return jnp.asarray(p, dtype=BF16)


_STAGE1_BLOCKS = ([(0, 0)] + [(k1, part) for k1 in range(1, FFT_SLABS // 2) for part in (0, 1)]
                  + [(FFT_SLABS // 2, 0)])


def _stage1_table():
    k1 = jnp.asarray([b[0] for b in _STAGE1_BLOCKS], jnp.int32)[:, None]
    is_im = jnp.asarray([b[1] for b in _STAGE1_BLOCKS], jnp.int32)[:, None]
    m = jnp.arange(FFT_SLABS, dtype=jnp.int32)[None, :]
    ang = ((k1 * m) % FFT_SLABS).astype(F32) * (2.0 * jnp.pi / FFT_SLABS)
    coef = jnp.where(is_im == 1, -jnp.sin(ang), jnp.cos(ang))
    return jnp.kron(coef, jnp.eye(BF16_ROWS, dtype=F32)).astype(BF16)


def _fftmix_kernel(x_ref, m1_ref, g_ref, wf_ref, p_ref, sg_ref, o_ref, a_ref, ym_ref, *, tc):
    half = FFT_SLABS // 2
    c = B_GROUP_DIM

    for grp in range(FFT_SLAB // BF16_ROWS):
        r0 = grp * BF16_ROWS
        xg = jnp.concatenate([x_ref[m * FFT_SLAB + r0:m * FFT_SLAB + r0 + BF16_ROWS, :]
                              for m in range(FFT_SLABS)], axis=0)
        s1 = jnp.dot(m1_ref[...], xg, preferred_element_type=F32).astype(BF16)
        for blk, (k1, part) in enumerate(_STAGE1_BLOCKS):
            a_ref[k1, part * FFT_SLAB + r0:part * FFT_SLAB + r0 + BF16_ROWS, :] = (
                s1[blk * BF16_ROWS:(blk + 1) * BF16_ROWS])

    for k1 in range(FFT_SLABS):
        if k1 % half == 0:
            y = jnp.dot(g_ref[k1, :, :FFT_SLAB], a_ref[k1, :FFT_SLAB, :], preferred_element_type=F32)
        else:
            a = a_ref[k1 if k1 < half else FFT_SLABS - k1]
            y = jnp.dot(g_ref[k1], a, preferred_element_type=F32)
        y = y.astype(BF16)
        for g in range(tc // c):
            cols = slice(g * c, (g + 1) * c)
            mix = jnp.dot(y[:FFT_SLAB, cols], wf_ref[g, :c, :], preferred_element_type=F32)
            mix += jnp.dot(y[FFT_SLAB:, cols], wf_ref[g, c:, :], preferred_element_type=F32)
            mix = mix.astype(BF16)
            for k3 in range(FFT_SLABS):
                ym_ref[k3, k1 * FFT_SLABS:(k1 + 1) * FFT_SLABS, cols] = (
                    mix[k3 * FFT_SLABS:(k3 + 1) * FFT_SLABS])

    for k3 in range(FFT_SLABS):
        rows = slice(k3 * FFT_SLAB, (k3 + 1) * FFT_SLAB)
        z = jnp.dot(p_ref[...], ym_ref[k3], preferred_element_type=F32)
        o_ref[rows, :] = (z * sg_ref[rows, :].astype(F32)).astype(o_ref.dtype)


def fftmix(xb, wf, sg, bsz, tc=512):
    e = xb.shape[1]
    gpt = tc // B_GROUP_DIM
    tile = pl.BlockSpec((None, SEQ, tc), lambda b, j: (b, 0, j))
    once = pl.Buffered(1)
    out = pl.pallas_call(
        functools.partial(_fftmix_kernel, tc=tc),
        grid=(bsz, e // tc),
        in_specs=[tile,
                  pl.BlockSpec((FFT_SLAB, FFT_SLAB), lambda b, j: (0, 0), pipeline_mode=once),
                  pl.BlockSpec((FFT_SLABS, 2 * FFT_SLAB, 2 * FFT_SLAB), lambda b, j: (0, 0, 0),
                               pipeline_mode=once),
                  pl.BlockSpec((gpt, 2 * B_GROUP_DIM, B_GROUP_DIM), lambda b, j: (j, 0, 0)),
                  pl.BlockSpec((FFT_SLAB, FFT_SLAB), lambda b, j: (0, 0), pipeline_mode=once),
                  tile],
        out_specs=tile,
        out_shape=jax.ShapeDtypeStruct((bsz, SEQ, e), BF16),
        scratch_shapes=[pltpu.VMEM((FFT_SLABS // 2 + 1, 2 * FFT_SLAB, tc), BF16),
                        pltpu.VMEM((FFT_SLABS, FFT_SLAB, tc), BF16)],
        compiler_params=_params("parallel", "parallel"),
        name="fftmix",
    )(xb.reshape(bsz, SEQ, e), _stage1_table(), _slab_dft_tables(), wf, _interleave_table(),
      sg.reshape(bsz, SEQ, e))
    return out.reshape(bsz * SEQ, e)


POOL_ROWS = 128
POOL_K = 256


def _cmix_kernel(prev_ref, cur_ref, next_ref, h_ref, wg_ref, wm_ref, scale_ref, o_ref,
                 wgbf_ref, wmbf_ref, *, tm):
    g = pl.program_id(0)
    i = pl.program_id(1)

    @pl.when(i == 0)
    def _():
        wgbf_ref[...] = wg_ref[...].astype(BF16)
        wmbf_ref[...] = wm_ref[...].astype(BF16)

    win = jnp.left_shift(2, g)
    lo_off = win // 2
    hi_off = win - 1 - lo_off
    t0 = (i % (SEQ // tm)) * tm
    ext = jnp.concatenate([prev_ref[...].astype(BF16), cur_ref[...].astype(BF16),
                           next_ref[...].astype(BF16)], axis=0)
    sg = _act(jnp.dot(h_ref[...], wgbf_ref[...], preferred_element_type=F32), "silu")
    pooled = []
    for j in range(tm // POOL_ROWS):
        start = min(j * POOL_ROWS, tm + 2 * HALO - POOL_K)
        t = t0 + j * POOL_ROWS + lax.broadcasted_iota(jnp.int32, (POOL_ROWS, 1), 0)
        u = t0 + (start - HALO) + lax.broadcasted_iota(jnp.int32, (1, POOL_K), 1)
        lo = jnp.maximum(t - lo_off, 0)
        hi = jnp.minimum(t + hi_off, SEQ - 1)
        band = jnp.where((u >= lo) & (u <= hi), 1.0, 0.0).astype(BF16)
        wsum = jnp.dot(band, ext[start:start + POOL_K], preferred_element_type=F32)
        cnt = (hi - lo + 1).astype(F32)
        rows = slice(j * POOL_ROWS, (j + 1) * POOL_ROWS)
        pooled.append((wsum / cnt - cur_ref[rows, :]).astype(BF16))
    p = jnp.concatenate(pooled, axis=0)
    acc = jnp.dot(p, wmbf_ref[...], preferred_element_type=F32)
    o_ref[...] = (acc * scale_ref[...] * sg).astype(o_ref.dtype)


def cmix(xc, h, w_in, w_mix, layer, gate_col0, scale, tm=512):
    m, e = xc.shape
    k = h.shape[1]
    c = C_GROUP_DIM
    gb0 = gate_col0 // c
    hb = tm // HALO
    last_hb = m // HALO - 1
    return pl.pallas_call(
        functools.partial(_cmix_kernel, tm=tm),
        grid=(C_GROUPS, m // tm),
        in_specs=[pl.BlockSpec((HALO, c), lambda g, i: (jnp.maximum(i * hb - 1, 0), g)),
                  pl.BlockSpec((tm, c), lambda g, i: (i, g)),
                  pl.BlockSpec((HALO, c), lambda g, i: (jnp.minimum((i + 1) * hb, last_hb), g)),
                  pl.BlockSpec((tm, k), lambda g, i: (i, 0)),
                  pl.BlockSpec((None, k, c), lambda g, i: (layer, 0, gb0 + g)),
                  pl.BlockSpec((None, None, c, c), lambda g, i: (layer, g, 0, 0)),
                  pl.BlockSpec((1, c), lambda g, i: (0, g))],
        out_specs=pl.BlockSpec((tm, c), lambda g, i: (i, g)),
        out_shape=jax.ShapeDtypeStruct((m, e), BF16),
        scratch_shapes=[pltpu.VMEM((k, c), BF16), pltpu.VMEM((c, c), BF16)],
        compiler_params=_params("arbitrary", "arbitrary"),
        name="cmix",
    )(xc, xc, xc, h, w_in, w_mix, scale.reshape(1, e))


def kernel(x, a_norm, a_w_in, a_v_gain, a_w_s, a_b_s, a_w_out, b_norm, b_w_in, b_w_mix, b_w_out,
           c_norm, c_w_in, c_w_mix, c_scale, c_w_out, final_norm):
    bsz, seq, d = x.shape
    assert (seq, d) == (SEQ, D_MODEL)
    e = D_INNER
    depth = 4
    norms = {0: a_norm, 1: b_norm, 2: c_norm}

    def gain_for(i):
        if i == depth:
            return final_norm
        return norms[i % 3][i // 3]

    xf = x.reshape(bsz * seq, d)
    h = rms_cast(xf, gain_for(0))
    for i in range(depth):
        kind, j = i % 3, i // 3
        if kind == 0:
            u, w_out = proj(h, a_w_in, j, 0, e, "gelu", BF16, cast=(a_w_out, j))
            gv, ssq = proj(h, a_w_in, j, e, e, "gelu", BF16, with_ssq=True)
            y = amix(h, a_w_in, j, 2 * e, u, gv, ssq, a_v_gain[j], a_w_s[j], a_b_s[j])
        elif kind == 1:
            xb, w_out = proj(h, b_w_in, j, 0, e, None, BF16, cast=(b_w_out, j))
            (sg,) = proj(h, b_w_in, j, e, e, "silu", BF16)
            wf = fold_channel_dft(b_w_mix, j)
            y = fftmix(xb, wf, sg, bsz)
        else:
            xc, w_out = proj(h, c_w_in, j, 0, e, None, F32, cast=(c_w_out, j))
            y = cmix(xc, h, c_w_in, c_w_mix, j, e, c_scale[j])
        last = i == depth - 1
        res = out_proj(y, w_out, xf, gain_for(i + 1), last)
        if last:
            return res.reshape(bsz, seq, d)
        xf, h = res
```

```python
import functools

import numpy as np
import jax
import jax.numpy as jnp
from jax import lax
from jax.experimental import pallas as pl
from jax.experimental.pallas import tpu as pltpu

D_MODEL = 2048
D_INNER = 4096
SEQ = 4096
CHUNK = 128
A_GROUPS = 8
A_GROUP_DIM = D_INNER // A_GROUPS
B_GROUPS = 8
B_GROUP_DIM = D_INNER // B_GROUPS
POOL_WINDOWS = (2, 4, 8, 16)
C_GROUPS = len(POOL_WINDOWS)
C_GROUP_DIM = D_INNER // C_GROUPS
EPS = 1e-6
HALO = 16

FFT_SLABS = 16
FFT_SLAB = SEQ // FFT_SLABS
BF16_ROWS = 16

VMEM_LIMIT = 56 * 1024 * 1024

F32 = jnp.float32
BF16 = jnp.bfloat16


def _params(*sem):
    return pltpu.CompilerParams(dimension_semantics=sem, vmem_limit_bytes=VMEM_LIMIT)


def _norm_kernel(x_ref, g_ref, h_ref):
    x = x_ref[...]
    r = lax.rsqrt(jnp.mean(x * x, axis=-1, keepdims=True) + EPS)
    h_ref[...] = (x * r * g_ref[...]).astype(h_ref.dtype)


def rms_cast(x, g, tm=512):
    m, d = x.shape
    return pl.pallas_call(
        _norm_kernel,
        grid=(m // tm,),
        in_specs=[pl.BlockSpec((tm, d), lambda i: (i, 0)),
                  pl.BlockSpec((1, d), lambda i: (0, 0))],
        out_specs=pl.BlockSpec((tm, d), lambda i: (i, 0)),
        out_shape=jax.ShapeDtypeStruct((m, d), BF16),
        compiler_params=_params("parallel"),
        name="rms_cast",
    )(x, g.reshape(1, d))


GELU_C = 0.7978845608028654
GELU_A = 0.044715


def _act(x, act):
    if act is None:
        return x
    hx = 0.5 * x
    if act == "gelu":
        t = jnp.tanh(x * (GELU_C + (GELU_C * GELU_A) * (x * x)))
    else:
        t = jnp.tanh(hx)
    return hx + hx * t


def _proj_kernel(a_ref, w_ref, *refs, act, with_ssq, with_cast):
    refs = list(refs)
    cast_in = refs.pop(0) if with_cast else None
    o_ref = refs.pop(0)
    ss_ref = refs.pop(0) if with_ssq else None
    cast_out = refs.pop(0) if with_cast else None
    (wbf_ref,) = refs

    @pl.when(pl.program_id(1) == 0)
    def _():
        wbf_ref[...] = w_ref[...].astype(BF16)

    val = _act(jnp.dot(a_ref[...], wbf_ref[...], preferred_element_type=F32), act)
    o_ref[...] = val.astype(o_ref.dtype)
    if with_ssq:
        ss_ref[...] = jnp.sum(val * val, axis=-1, keepdims=True)
    if with_cast:
        cast_out[...] = cast_in[...].astype(cast_out.dtype)


def proj(a, w, layer, col0, ncols, act, out_dtype, with_ssq=False, cast=None, tm=1024, tn=1024):
    m, k = a.shape
    nb0 = col0 // tn
    n_tiles = ncols // tn
    n_m = m // tm
    in_specs = [pl.BlockSpec((tm, k), lambda j, i: (i, 0)),
                pl.BlockSpec((None, k, tn), lambda j, i: (layer, 0, nb0 + j))]
    args = [a, w]
    out_specs = [pl.BlockSpec((tm, tn), lambda j, i: (i, j))]
    out_shape = [jax.ShapeDtypeStruct((m, ncols), out_dtype)]
    if with_ssq:
        out_specs.append(pl.BlockSpec((None, tm, 1), lambda j, i: (j, i, 0)))
        out_shape.append(jax.ShapeDtypeStruct((n_tiles, m, 1), F32))
    if cast is not None:
        cw, cl = cast
        _, ce, cd = cw.shape
        slab = ce // (n_tiles * n_m)
        assert slab * n_tiles * n_m == ce and slab % BF16_ROWS == 0
        in_specs.append(pl.BlockSpec((None, slab, cd), lambda j, i: (cl, j * n_m + i, 0)))
        args.append(cw)
        out_specs.append(pl.BlockSpec((slab, cd), lambda j, i: (j * n_m + i, 0)))
        out_shape.append(jax.ShapeDtypeStruct((ce, cd), BF16))
    return pl.pallas_call(
        functools.partial(_proj_kernel, act=act, with_ssq=with_ssq, with_cast=cast is not None),
        grid=(n_tiles, n_m),
        in_specs=in_specs,
        out_specs=out_specs,
        out_shape=out_shape,
        scratch_shapes=[pltpu.VMEM((k, tn), BF16)],
        compiler_params=_params("arbitrary", "arbitrary"),
        name="proj_" + str(act),
    )(*args)


def _out_kernel(y_ref, w_ref, x_ref, g_ref, *refs, n_tiles, tn, last):
    if last:
        o_ref, xrow = refs
    else:
        xo_ref, o_ref, xrow = refs
    n = pl.program_id(1)
    xn = x_ref[...] + jnp.dot(y_ref[...], w_ref[...], preferred_element_type=F32)
    if not last:
        xo_ref[...] = xn
    xrow[n] = xn

    @pl.when(n == n_tiles - 1)
    def _():
        ss = None
        for j in range(n_tiles):
            v = xrow[j]
            s = jnp.sum(v * v, axis=-1, keepdims=True)
            ss = s if ss is None else ss + s
        r = lax.rsqrt(ss * (1.0 / (n_tiles * tn)) + EPS)
        for j in range(n_tiles):
            cols = slice(j * tn, (j + 1) * tn)
            o_ref[:, cols] = (xrow[j] * r * g_ref[:, cols]).astype(o_ref.dtype)


def out_proj(y, w, x, g, last, tm=1024, tn=512):
    m, k = y.shape
    d = w.shape[1]
    n_tiles = d // tn
    in_specs = [pl.BlockSpec((tm, k), lambda i, n: (i, 0)),
                pl.BlockSpec((k, tn), lambda i, n: (0, n)),
                pl.BlockSpec((tm, tn), lambda i, n: (i, n)),
                pl.BlockSpec((1, d), lambda i, n: (0, 0))]
    row_spec = pl.BlockSpec((tm, d), lambda i, n: (i, 0))
    if last:
        out_specs = row_spec
        out_shape = jax.ShapeDtypeStruct((m, d), F32)
    else:
        out_specs = [pl.BlockSpec((tm, tn), lambda i, n: (i, n)), row_spec]
        out_shape = [jax.ShapeDtypeStruct((m, d), F32), jax.ShapeDtypeStruct((m, d), BF16)]
    return pl.pallas_call(
        functools.partial(_out_kernel, n_tiles=n_tiles, tn=tn, last=last),
        grid=(m // tm, n_tiles),
        in_specs=in_specs,
        out_specs=out_specs,
        out_shape=out_shape,
        scratch_shapes=[pltpu.VMEM((n_tiles, tm, tn), F32)],
        compiler_params=_params("parallel", "arbitrary"),
        name="out_proj",
    )(y, w, x, g.reshape(1, d))


AMIX_GROUPS_PER_STEP = 2


def _amix_kernel(h_ref, w_ref, u_ref, gv_ref, ss_ref, gain_ref, ws_ref, b_ref, o_ref, wbf_ref, t_ref,
                 *, tm):
    i, s = pl.program_id(0), pl.program_id(1)
    gps, c = AMIX_GROUPS_PER_STEP, A_GROUP_DIM

    @pl.when(i == 0)
    def _():
        for gg in range(gps):
            wbf_ref[s * gps + gg] = w_ref[:, gg * c:(gg + 1) * c].astype(BF16)

    ssq = ss_ref[0]
    for t in range(1, ss_ref.shape[0]):
        ssq = ssq + ss_ref[t]
    r = lax.rsqrt(ssq * (1.0 / D_INNER) + EPS)
    for gg in range(gps):
        cols = slice(gg * c, (gg + 1) * c)
        vn = (gv_ref[:, cols].astype(F32) * r * gain_ref[:, cols]).astype(BF16)
        wg = ws_ref[gg]
        bias = b_ref[gg]
        for ch in range(tm // CHUNK):
            rows = slice(ch * CHUNK, (ch + 1) * CHUNK)
            sv = jnp.dot(wg, vn[rows], preferred_element_type=F32) + bias
            t_ref[rows, cols] = u_ref[rows, cols].astype(F32) * sv
        sg = _act(jnp.dot(h_ref[...], wbf_ref[s * gps + gg], preferred_element_type=F32), "silu")
        o_ref[:, cols] = (t_ref[:, cols] * sg).astype(o_ref.dtype)


def amix(h, w_in, layer, gate_col0, u, gv, ssq, gain, w_s, b_s, tm=1024):
    m, e = u.shape
    k = h.shape[1]
    gps = AMIX_GROUPS_PER_STEP
    c = gps * A_GROUP_DIM
    n_steps = A_GROUPS // gps
    gb0 = gate_col0 // c
    n_ss = ssq.shape[0]
    grp = pl.BlockSpec((tm, c), lambda i, s: (i, s))
    return pl.pallas_call(
        functools.partial(_amix_kernel, tm=tm),
        grid=(m // tm, n_steps),
        in_specs=[pl.BlockSpec((tm, k), lambda i, s: (i, 0)),
                  pl.BlockSpec((None, k, c),
                               lambda i, s: (layer, 0, gb0 + jnp.where(i == 0, s, n_steps - 1)),
                               pipeline_mode=pl.Buffered(1)),
                  grp, grp,
                  pl.BlockSpec((n_ss, tm, 1), lambda i, s: (0, i, 0)),
                  pl.BlockSpec((1, c), lambda i, s: (0, s)),
                  pl.BlockSpec((gps, CHUNK, CHUNK), lambda i, s: (s, 0, 0)),
                  pl.BlockSpec((gps, CHUNK, 1), lambda i, s: (s, 0, 0))],
        out_specs=grp,
        out_shape=jax.ShapeDtypeStruct((m, e), BF16),
        scratch_shapes=[pltpu.VMEM((A_GROUPS, k, A_GROUP_DIM), BF16), pltpu.VMEM((tm, c), F32)],
        compiler_params=_params("arbitrary", "arbitrary"),
        name="amix",
    )(h, w_in, u, gv, ssq, gain.reshape(1, e), w_s.astype(BF16), b_s.reshape(A_GROUPS, CHUNK, 1))


def _dft_tables(n):
    idx = jnp.arange(n, dtype=jnp.int32)
    ang = ((idx[:, None] * idx[None, :]) % n).astype(F32) * (2.0 * jnp.pi / n)
    return jnp.cos(ang), jnp.sin(ang)


def _fold_kernel(cs_ref, w_ref, o_ref, *, scale):
    acc = jnp.dot(cs_ref[...], w_ref[...].astype(BF16), preferred_element_type=F32)
    o_ref[...] = (acc * scale).astype(o_ref.dtype)


def fold_channel_dft(w_mix, layer):
    _, g, c, _ = w_mix.shape
    cc, sc = _dft_tables(c)
    cs = jnp.concatenate([cc, sc], axis=0).astype(BF16)
    scale = float((SEQ * c) ** -0.5)
    return pl.pallas_call(
        functools.partial(_fold_kernel, scale=scale),
        grid=(g,),
        in_specs=[pl.BlockSpec((2 * c, c), lambda i: (0, 0)),
                  pl.BlockSpec((None, None, c, c), lambda i: (layer, i, 0, 0))],
        out_specs=pl.BlockSpec((None, 2 * c, c), lambda i: (i, 0, 0)),
        out_shape=jax.ShapeDtypeStruct((g, 2 * c, c), BF16),
        compiler_params=_params("parallel"),
        name="fold_channel_dft",
    )(cs, w_mix)


def _slab_dft_tables():
    k1 = jnp.arange(FFT_SLABS // 2 + 1, dtype=jnp.int32)[:, None, None]
    j = jnp.arange(FFT_SLAB, dtype=jnp.int32)[None, :, None]
    n2 = jnp.arange(FFT_SLAB, dtype=jnp.int32)[None, None, :]
    a1 = ((k1 * n2) % SEQ).astype(F32) * (2.0 * jnp.pi / SEQ)
    a2 = ((j * n2) % FFT_SLAB).astype(F32) * (2.0 * jnp.pi / FFT_SLAB)
    c1, s1, c2, s2 = jnp.cos(a1), jnp.sin(a1), jnp.cos(a2), jnp.sin(a2)
    cos, sin = c1 * c2 - s1 * s2, s1 * c2 + c1 * s2
    top = jnp.concatenate([cos, sin], axis=2)
    bot = jnp.concatenate([-sin, cos], axis=2)
    return jnp.concatenate([top, bot], axis=1).astype(BF16)


def _interleave_table():
    k1, r = np.divmod(np.arange(FFT_SLAB), FFT_SLABS)
    k2 = np.where(k1 <= FFT_SLABS // 2, r, FFT_SLABS - 1 - r)
    p = np.zeros((FFT_SLAB, FFT_SLAB), np.float32)
    p[FFT_SLABS * k2 + k1, np.arange(FFT_SLAB)] = 1.0
    return jnp.asarray(p, dtype=BF16)


_STAGE1_BLOCKS = ([(0, 0)] + [(k1, part) for k1 in range(1, FFT_SLABS // 2) for part in (0, 1)]
                  + [(FFT_SLABS // 2, 0)])


def _stage1_table():
    k1 = jnp.asarray([b[0] for b in _STAGE1_BLOCKS], jnp.int32)[:, None]
    is_im = jnp.asarray([b[1] for b in _STAGE1_BLOCKS], jnp.int32)[:, None]
    m = jnp.arange(FFT_SLABS, dtype=jnp.int32)[None, :]
    ang = ((k1 * m) % FFT_SLABS).astype(F32) * (2.0 * jnp.pi / FFT_SLABS)
    coef = jnp.where(is_im == 1, -jnp.sin(ang), jnp.cos(ang))
    return jnp.kron(coef, jnp.eye(BF16_ROWS, dtype=F32)).astype(BF16)


def _fftmix_kernel(x_ref, m1_ref, g_ref, wf_ref, p_ref, sg_ref, o_ref, a_ref, ym_ref, *, tc):
    half = FFT_SLABS // 2
    c = B_GROUP_DIM

    for grp in range(FFT_SLAB // BF16_ROWS):
        r0 = grp * BF16_ROWS
        xg = jnp.concatenate([x_ref[m * FFT_SLAB + r0:m * FFT_SLAB + r0 + BF16_ROWS, :]
                              for m in range(FFT_SLABS)], axis=0)
        s1 = jnp.dot(m1_ref[...], xg, preferred_element_type=F32).astype(BF16)
        for blk, (k1, part) in enumerate(_STAGE1_BLOCKS):
            a_ref[k1, part * FFT_SLAB + r0:part * FFT_SLAB + r0 + BF16_ROWS, :] = (
                s1[blk * BF16_ROWS:(blk + 1) * BF16_ROWS])

    for k1 in range(half + 1):
        if k1 % half == 0:
            y = jnp.dot(g_ref[k1, :, :FFT_SLAB], a_ref[k1, :FFT_SLAB, :], preferred_element_type=F32)
        else:
            y = jnp.dot(g_ref[k1], a_ref[k1], preferred_element_type=F32)
        y = y.astype(BF16)
        for g in range(tc // c):
            cols = slice(g * c, (g + 1) * c)
            p_re = jnp.dot(y[:FFT_SLAB, cols], wf_ref[g, :c, :], preferred_element_type=F32)
            p_im = jnp.dot(y[FFT_SLAB:, cols], wf_ref[g, c:, :], preferred_element_type=F32)
            fwd = (p_re + p_im).astype(BF16)
            for k3 in range(FFT_SLABS):
                ym_ref[k3, k1 * FFT_SLABS:(k1 + 1) * FFT_SLABS, cols] = (
                    fwd[k3 * FFT_SLABS:(k3 + 1) * FFT_SLABS])
            if k1 % half != 0:
                mir = (p_re - p_im).astype(BF16)
                km = FFT_SLABS - k1
                for k3 in range(FFT_SLABS):
                    src = FFT_SLABS - 1 - k3
                    ym_ref[k3, km * FFT_SLABS:(km + 1) * FFT_SLABS, cols] = (
                        mir[src * FFT_SLABS:(src + 1) * FFT_SLABS])

    for k3 in range(FFT_SLABS):
        rows = slice(k3 * FFT_SLAB, (k3 + 1) * FFT_SLAB)
        z = jnp.dot(p_ref[...], ym_ref[k3], preferred_element_type=F32)
        o_ref[rows, :] = (z * sg_ref[rows, :].astype(F32)).astype(o_ref.dtype)


def fftmix(xb, wf, sg, bsz, tc=512):
    e = xb.shape[1]
    gpt = tc // B_GROUP_DIM
    tile = pl.BlockSpec((None, SEQ, tc), lambda b, j: (b, 0, j))
    once = pl.Buffered(1)
    out = pl.pallas_call(
        functools.partial(_fftmix_kernel, tc=tc),
        grid=(bsz, e // tc),
        in_specs=[tile,
                  pl.BlockSpec((FFT_SLAB, FFT_SLAB), lambda b, j: (0, 0), pipeline_mode=once),
                  pl.BlockSpec((FFT_SLABS // 2 + 1, 2 * FFT_SLAB, 2 * FFT_SLAB), lambda b, j: (0, 0, 0),
                               pipeline_mode=once),
                  pl.BlockSpec((gpt, 2 * B_GROUP_DIM, B_GROUP_DIM), lambda b, j: (j, 0, 0)),
                  pl.BlockSpec((FFT_SLAB, FFT_SLAB), lambda b, j: (0, 0), pipeline_mode=once),
                  tile],
        out_specs=tile,
        out_shape=jax.ShapeDtypeStruct((bsz, SEQ, e), BF16),
        scratch_shapes=[pltpu.VMEM((FFT_SLABS // 2 + 1, 2 * FFT_SLAB, tc), BF16),
                        pltpu.VMEM((FFT_SLABS, FFT_SLAB, tc), BF16)],
        compiler_params=_params("parallel", "parallel"),
        name="fftmix",
    )(xb.reshape(bsz, SEQ, e), _stage1_table(), _slab_dft_tables(), wf, _interleave_table(),
      sg.reshape(bsz, SEQ, e))
    return out.reshape(bsz * SEQ, e)


POOL_ROWS = 128
POOL_K = 256


def _cmix_kernel(prev_ref, cur_ref, next_ref, h_ref, wg_ref, wm_ref, scale_ref, o_ref,
                 wgbf_ref, wmbf_ref, *, tm):
    g = pl.program_id(0)
    i = pl.program_id(1)

    @pl.when(i == 0)
    def _():
        wgbf_ref[...] = wg_ref[...].astype(BF16)
        wmbf_ref[...] = wm_ref[...].astype(BF16)

    win = jnp.left_shift(2, g)
    lo_off = win // 2
    hi_off = win - 1 - lo_off
    t0 = (i % (SEQ // tm)) * tm
    ext = jnp.concatenate([prev_ref[...].astype(BF16), cur_ref[...].astype(BF16),
                           next_ref[...].astype(BF16)], axis=0)
    sg = _act(jnp.dot(h_ref[...], wgbf_ref[...], preferred_element_type=F32), "silu")
    pooled = []
    for j in range(tm // POOL_ROWS):
        start = min(j * POOL_ROWS, tm + 2 * HALO - POOL_K)
        t = t0 + j * POOL_ROWS + lax.broadcasted_iota(jnp.int32, (POOL_ROWS, 1), 0)
        u = t0 + (start - HALO) + lax.broadcasted_iota(jnp.int32, (1, POOL_K), 1)
        lo = jnp.maximum(t - lo_off, 0)
        hi = jnp.minimum(t + hi_off, SEQ - 1)
        band = jnp.where((u >= lo) & (u <= hi), 1.0, 0.0).astype(BF16)
        wsum = jnp.dot(band, ext[start:start + POOL_K], preferred_element_type=F32)
        cnt = (hi - lo + 1).astype(F32)
        rows = slice(j * POOL_ROWS, (j + 1) * POOL_ROWS)
        pooled.append((wsum / cnt - cur_ref[rows, :]).astype(BF16))
    p = jnp.concatenate(pooled, axis=0)
    acc = jnp.dot(p, wmbf_ref[...], preferred_element_type=F32)
    o_ref[...] = (acc * scale_ref[...] * sg).astype(o_ref.dtype)


def cmix(xc, h, w_in, w_mix, layer, gate_col0, scale, tm=512):
    m, e = xc.shape
    k = h.shape[1]
    c = C_GROUP_DIM
    gb0 = gate_col0 // c
    hb = tm // HALO
    last_hb = m // HALO - 1
    return pl.pallas_call(
        functools.partial(_cmix_kernel, tm=tm),
        grid=(C_GROUPS, m // tm),
        in_specs=[pl.BlockSpec((HALO, c), lambda g, i: (jnp.maximum(i * hb - 1, 0), g)),
                  pl.BlockSpec((tm, c), lambda g, i: (i, g)),
                  pl.BlockSpec((HALO, c), lambda g, i: (jnp.minimum((i + 1) * hb, last_hb), g)),
                  pl.BlockSpec((tm, k), lambda g, i: (i, 0)),
                  pl.BlockSpec((None, k, c), lambda g, i: (layer, 0, gb0 + g)),
                  pl.BlockSpec((None, None, c, c), lambda g, i: (layer, g, 0, 0)),
                  pl.BlockSpec((1, c), lambda g, i: (0, g))],
        out_specs=pl.BlockSpec((tm, c), lambda g, i: (i, g)),
        out_shape=jax.ShapeDtypeStruct((m, e), BF16),
        scratch_shapes=[pltpu.VMEM((k, c), BF16), pltpu.VMEM((c, c), BF16)],
        compiler_params=_params("arbitrary", "arbitrary"),
        name="cmix",
    )(xc, xc, xc, h, w_in, w_mix, scale.reshape(1, e))


def kernel(x, a_norm, a_w_in, a_v_gain, a_w_s, a_b_s, a_w_out, b_norm, b_w_in, b_w_mix, b_w_out,
           c_norm, c_w_in, c_w_mix, c_scale, c_w_out, final_norm):
    bsz, seq, d = x.shape
    assert (seq, d) == (SEQ, D_MODEL)
    e = D_INNER
    depth = 4
    norms = {0: a_norm, 1: b_norm, 2: c_norm}

    def gain_for(i):
        if i == depth:
            return final_norm
        return norms[i % 3][i // 3]

    xf = x.reshape(bsz * seq, d)
    h = rms_cast(xf, gain_for(0))
    for i in range(depth):
        kind, j = i % 3, i // 3
        if kind == 0:
            u, w_out = proj(h, a_w_in, j, 0, e, "gelu", BF16, cast=(a_w_out, j))
            gv, ssq = proj(h, a_w_in, j, e, e, "gelu", BF16, with_ssq=True)
            y = amix(h, a_w_in, j, 2 * e, u, gv, ssq, a_v_gain[j], a_w_s[j], a_b_s[j])
        elif kind == 1:
            xb, w_out = proj(h, b_w_in, j, 0, e, None, BF16, cast=(b_w_out, j))
            (sg,) = proj(h, b_w_in, j, e, e, "silu", BF16)
            wf = fold_channel_dft(b_w_mix, j)
            y = fftmix(xb, wf, sg, bsz)
        else:
            xc, w_out = proj(h, c_w_in, j, 0, e, None, F32, cast=(c_w_out, j))
            y = cmix(xc, h, c_w_in, c_w_mix, j, e, c_scale[j])
        last = i == depth - 1
        res = out_proj(y, w_out, xf, gain_for(i + 1), last)
        if last:
            return res.reshape(bsz, seq, d)
        xf, h = res
```

```python
import functools

import numpy as np
import jax
import jax.numpy as jnp
from jax import lax
from jax.experimental import pallas as pl
from jax.experimental.pallas import tpu as pltpu

D_MODEL = 2048
D_INNER = 4096
SEQ = 4096
CHUNK = 128
A_GROUPS = 8
A_GROUP_DIM = D_INNER // A_GROUPS
B_GROUPS = 8
B_GROUP_DIM = D_INNER // B_GROUPS
POOL_WINDOWS = (2, 4, 8, 16)
C_GROUPS = len(POOL_WINDOWS)
C_GROUP_DIM = D_INNER // C_GROUPS
EPS = 1e-6
HALO = 16

FFT_SLABS = 16
FFT_SLAB = SEQ // FFT_SLABS
BF16_ROWS = 16

VMEM_LIMIT = 56 * 1024 * 1024

F32 = jnp.float32
BF16 = jnp.bfloat16


def _params(*sem):
    return pltpu.CompilerParams(dimension_semantics=sem, vmem_limit_bytes=VMEM_LIMIT)


def _norm_kernel(x_ref, g_ref, h_ref):
    x = x_ref[...]
    r = lax.rsqrt(jnp.mean(x * x, axis=-1, keepdims=True) + EPS)
    h_ref[...] = (x * r * g_ref[...]).astype(h_ref.dtype)


def rms_cast(x, g, tm=512):
    m, d = x.shape
    return pl.pallas_call(
        _norm_kernel,
        grid=(m // tm,),
        in_specs=[pl.BlockSpec((tm, d), lambda i: (i, 0)),
                  pl.BlockSpec((1, d), lambda i: (0, 0))],
        out_specs=pl.BlockSpec((tm, d), lambda i: (i, 0)),
        out_shape=jax.ShapeDtypeStruct((m, d), BF16),
        compiler_params=_params("parallel"),
        name="rms_cast",
    )(x, g.reshape(1, d))


GELU_C = 0.7978845608028654
GELU_A = 0.044715


def _act(x, act):
    if act is None:
        return x
    hx = 0.5 * x
    if act == "gelu":
        t = jnp.tanh(x * (GELU_C + (GELU_C * GELU_A) * (x * x)))
    else:
        t = jnp.tanh(hx)
    return hx + hx * t


def _proj_kernel(a_ref, w_ref, *refs, act, with_ssq, with_cast):
    refs = list(refs)
    cast_in = refs.pop(0) if with_cast else None
    o_ref = refs.pop(0)
    ss_ref = refs.pop(0) if with_ssq else None
    cast_out = refs.pop(0) if with_cast else None
    (wbf_ref,) = refs

    @pl.when(pl.program_id(1) == 0)
    def _():
        wbf_ref[...] = w_ref[...].astype(BF16)

    val = _act(jnp.dot(a_ref[...], wbf_ref[...], preferred_element_type=F32), act)
    o_ref[...] = val.astype(o_ref.dtype)
    if with_ssq:
        ss_ref[...] = jnp.sum(val * val, axis=-1, keepdims=True)
    if with_cast:
        cast_out[...] = cast_in[...].astype(cast_out.dtype)


def proj(a, w, layer, col0, ncols, act, out_dtype, with_ssq=False, cast=None, tm=1024, tn=1024):
    m, k = a.shape
    nb0 = col0 // tn
    n_tiles = ncols // tn
    n_m = m // tm
    in_specs = [pl.BlockSpec((tm, k), lambda j, i: (i, 0)),
                pl.BlockSpec((None, k, tn), lambda j, i: (layer, 0, nb0 + j))]
    args = [a, w]
    out_specs = [pl.BlockSpec((tm, tn), lambda j, i: (i, j))]
    out_shape = [jax.ShapeDtypeStruct((m, ncols), out_dtype)]
    if with_ssq:
        out_specs.append(pl.BlockSpec((None, tm, 1), lambda j, i: (j, i, 0)))
        out_shape.append(jax.ShapeDtypeStruct((n_tiles, m, 1), F32))
    if cast is not None:
        cw, cl = cast
        _, ce, cd = cw.shape
        slab = ce // (n_tiles * n_m)
        assert slab * n_tiles * n_m == ce and slab % BF16_ROWS == 0
        in_specs.append(pl.BlockSpec((None, slab, cd), lambda j, i: (cl, j * n_m + i, 0)))
        args.append(cw)
        out_specs.append(pl.BlockSpec((slab, cd), lambda j, i: (j * n_m + i, 0)))
        out_shape.append(jax.ShapeDtypeStruct((ce, cd), BF16))
    return pl.pallas_call(
        functools.partial(_proj_kernel, act=act, with_ssq=with_ssq, with_cast=cast is not None),
        grid=(n_tiles, n_m),
        in_specs=in_specs,
        out_specs=out_specs,
        out_shape=out_shape,
        scratch_shapes=[pltpu.VMEM((k, tn), BF16)],
        compiler_params=_params("arbitrary", "arbitrary"),
        name="proj_" + str(act),
    )(*args)


def _out_kernel(y_ref, w_ref, x_ref, g_ref, *refs, n_tiles, tn, last):
    if last:
        o_ref, xrow = refs
    else:
        xo_ref, o_ref, xrow = refs
    n = pl.program_id(1)
    xn = x_ref[...] + jnp.dot(y_ref[...], w_ref[...], preferred_element_type=F32)
    if not last:
        xo_ref[...] = xn
    xrow[n] = xn

    @pl.when(n == n_tiles - 1)
    def _():
        ss = None
        for j in range(n_tiles):
            v = xrow[j]
            s = jnp.sum(v * v, axis=-1, keepdims=True)
            ss = s if ss is None else ss + s
        r = lax.rsqrt(ss * (1.0 / (n_tiles * tn)) + EPS)
        for j in range(n_tiles):
            cols = slice(j * tn, (j + 1) * tn)
            o_ref[:, cols] = (xrow[j] * r * g_ref[:, cols]).astype(o_ref.dtype)


def out_proj(y, w, x, g, last, tm=1024, tn=512):
    m, k = y.shape
    d = w.shape[1]
    n_tiles = d // tn
    in_specs = [pl.BlockSpec((tm, k), lambda i, n: (i, 0)),
                pl.BlockSpec((k, tn), lambda i, n: (0, n)),
                pl.BlockSpec((tm, tn), lambda i, n: (i, n)),
                pl.BlockSpec((1, d), lambda i, n: (0, 0))]
    row_spec = pl.BlockSpec((tm, d), lambda i, n: (i, 0))
    if last:
        out_specs = row_spec
        out_shape = jax.ShapeDtypeStruct((m, d), F32)
    else:
        out_specs = [pl.BlockSpec((tm, tn), lambda i, n: (i, n)), row_spec]
        out_shape = [jax.ShapeDtypeStruct((m, d), F32), jax.ShapeDtypeStruct((m, d), BF16)]
    return pl.pallas_call(
        functools.partial(_out_kernel, n_tiles=n_tiles, tn=tn, last=last),
        grid=(m // tm, n_tiles),
        in_specs=in_specs,
        out_specs=out_specs,
        out_shape=out_shape,
        scratch_shapes=[pltpu.VMEM((n_tiles, tm, tn), F32)],
        compiler_params=_params("parallel", "arbitrary"),
        name="out_proj",
    )(y, w, x, g.reshape(1, d))


AMIX_GROUPS_PER_STEP = 2


def _amix_kernel(h_ref, w_ref, u_ref, gv_ref, ss_ref, gain_ref, ws_ref, b_ref, o_ref, wbf_ref, t_ref,
                 *, tm):
    i, s = pl.program_id(0), pl.program_id(1)
    gps, c = AMIX_GROUPS_PER_STEP, A_GROUP_DIM

    @pl.when(i == 0)
    def _():
        for gg in range(gps):
            wbf_ref[s * gps + gg] = w_ref[:, gg * c:(gg + 1) * c].astype(BF16)

    ssq = ss_ref[0]
    for t in range(1, ss_ref.shape[0]):
        ssq = ssq + ss_ref[t]
    r = lax.rsqrt(ssq * (1.0 / D_INNER) + EPS)
    for gg in range(gps):
        cols = slice(gg * c, (gg + 1) * c)
        vn = (gv_ref[:, cols].astype(F32) * r * gain_ref[:, cols]).astype(BF16)
        wg = ws_ref[gg]
        bias = b_ref[gg]
        for ch in range(tm // CHUNK):
            rows = slice(ch * CHUNK, (ch + 1) * CHUNK)
            sv = jnp.dot(wg, vn[rows], preferred_element_type=F32) + bias
            t_ref[rows, cols] = _act(u_ref[rows, cols].astype(F32), "gelu") * sv
        sg = _act(jnp.dot(h_ref[...], wbf_ref[s * gps + gg], preferred_element_type=F32), "silu")
        o_ref[:, cols] = (t_ref[:, cols] * sg).astype(o_ref.dtype)


def amix(h, w_in, layer, gate_col0, u, gv, ssq, gain, w_s, b_s, tm=1024):
    m, e = u.shape
    k = h.shape[1]
    gps = AMIX_GROUPS_PER_STEP
    c = gps * A_GROUP_DIM
    n_steps = A_GROUPS // gps
    gb0 = gate_col0 // c
    n_ss = ssq.shape[0]
    grp = pl.BlockSpec((tm, c), lambda i, s: (i, s))
    return pl.pallas_call(
        functools.partial(_amix_kernel, tm=tm),
        grid=(m // tm, n_steps),
        in_specs=[pl.BlockSpec((tm, k), lambda i, s: (i, 0)),
                  pl.BlockSpec((None, k, c),
                               lambda i, s: (layer, 0, gb0 + jnp.where(i == 0, s, n_steps - 1)),
                               pipeline_mode=pl.Buffered(1)),
                  grp, grp,
                  pl.BlockSpec((n_ss, tm, 1), lambda i, s: (0, i, 0)),
                  pl.BlockSpec((1, c), lambda i, s: (0, s)),
                  pl.BlockSpec((gps, CHUNK, CHUNK), lambda i, s: (s, 0, 0)),
                  pl.BlockSpec((gps, CHUNK, 1), lambda i, s: (s, 0, 0))],
        out_specs=grp,
        out_shape=jax.ShapeDtypeStruct((m, e), BF16),
        scratch_shapes=[pltpu.VMEM((A_GROUPS, k, A_GROUP_DIM), BF16), pltpu.VMEM((tm, c), F32)],
        compiler_params=_params("arbitrary", "arbitrary"),
        name="amix",
    )(h, w_in, u, gv, ssq, gain.reshape(1, e), w_s.astype(BF16), b_s.reshape(A_GROUPS, CHUNK, 1))


def _dft_tables(n):
    idx = jnp.arange(n, dtype=jnp.int32)
    ang = ((idx[:, None] * idx[None, :]) % n).astype(F32) * (2.0 * jnp.pi / n)
    return jnp.cos(ang), jnp.sin(ang)


def _fold_kernel(cs_ref, w_ref, o_ref, *, scale):
    acc = jnp.dot(cs_ref[...], w_ref[...].astype(BF16), preferred_element_type=F32)
    o_ref[...] = (acc * scale).astype(o_ref.dtype)


def fold_channel_dft(w_mix, layer):
    _, g, c, _ = w_mix.shape
    cc, sc = _dft_tables(c)
    cs = jnp.concatenate([cc, sc], axis=0).astype(BF16)
    scale = float((SEQ * c) ** -0.5)
    return pl.pallas_call(
        functools.partial(_fold_kernel, scale=scale),
        grid=(g,),
        in_specs=[pl.BlockSpec((2 * c, c), lambda i: (0, 0)),
                  pl.BlockSpec((None, None, c, c), lambda i: (layer, i, 0, 0))],
        out_specs=pl.BlockSpec((None, 2 * c, c), lambda i: (i, 0, 0)),
        out_shape=jax.ShapeDtypeStruct((g, 2 * c, c), BF16),
        compiler_params=_params("parallel"),
        name="fold_channel_dft",
    )(cs, w_mix)


def _slab_dft_tables():
    k1 = jnp.arange(FFT_SLABS // 2 + 1, dtype=jnp.int32)[:, None, None]
    j = jnp.arange(FFT_SLAB, dtype=jnp.int32)[None, :, None]
    n2 = jnp.arange(FFT_SLAB, dtype=jnp.int32)[None, None, :]
    a1 = ((k1 * n2) % SEQ).astype(F32) * (2.0 * jnp.pi / SEQ)
    a2 = ((j * n2) % FFT_SLAB).astype(F32) * (2.0 * jnp.pi / FFT_SLAB)
    c1, s1, c2, s2 = jnp.cos(a1), jnp.sin(a1), jnp.cos(a2), jnp.sin(a2)
    cos, sin = c1 * c2 - s1 * s2, s1 * c2 + c1 * s2
    top = jnp.concatenate([cos, sin], axis=2)
    bot = jnp.concatenate([-sin, cos], axis=2)
    return jnp.concatenate([top, bot], axis=1).astype(BF16)


def _interleave_table():
    k1, r = np.divmod(np.arange(FFT_SLAB), FFT_SLABS)
    k2 = np.where(k1 <= FFT_SLABS // 2, r, FFT_SLABS - 1 - r)
    p = np.zeros((FFT_SLAB, FFT_SLAB), np.float32)
    p[FFT_SLABS * k2 + k1, np.arange(FFT_SLAB)] = 1.0
    return jnp.asarray(p, dtype=BF16)


_STAGE1_BLOCKS = ([(0, 0)] + [(k1, part) for k1 in range(1, FFT_SLABS // 2) for part in (0, 1)]
                  + [(FFT_SLABS // 2, 0)])


def _stage1_table():
    k1 = jnp.asarray([b[0] for b in _STAGE1_BLOCKS], jnp.int32)[:, None]
    is_im = jnp.asarray([b[1] for b in _STAGE1_BLOCKS], jnp.int32)[:, None]
    m = jnp.arange(FFT_SLABS, dtype=jnp.int32)[None, :]
    ang = ((k1 * m) % FFT_SLABS).astype(F32) * (2.0 * jnp.pi / FFT_SLABS)
    coef = jnp.where(is_im == 1, -jnp.sin(ang), jnp.cos(ang))
    return jnp.kron(coef, jnp.eye(BF16_ROWS, dtype=F32)).astype(BF16)


def _fftmix_kernel(x_ref, m1_ref, g_ref, wf_ref, p_ref, sg_ref, o_ref, a_ref, ym_ref, *, tc):
    half = FFT_SLABS // 2
    c = B_GROUP_DIM

    for grp in range(FFT_SLAB // BF16_ROWS):
        r0 = grp * BF16_ROWS
        xg = jnp.concatenate([x_ref[m * FFT_SLAB + r0:m * FFT_SLAB + r0 + BF16_ROWS, :]
                              for m in range(FFT_SLABS)], axis=0)
        s1 = jnp.dot(m1_ref[...], xg, preferred_element_type=F32).astype(BF16)
        for blk, (k1, part) in enumerate(_STAGE1_BLOCKS):
            a_ref[k1, part * FFT_SLAB + r0:part * FFT_SLAB + r0 + BF16_ROWS, :] = (
                s1[blk * BF16_ROWS:(blk + 1) * BF16_ROWS])

    for k1 in range(half + 1):
        if k1 % half == 0:
            y = jnp.dot(g_ref[k1, :, :FFT_SLAB], a_ref[k1, :FFT_SLAB, :], preferred_element_type=F32)
        else:
            y = jnp.dot(g_ref[k1], a_ref[k1], preferred_element_type=F32)
        y = y.astype(BF16)
        for g in range(tc // c):
            cols = slice(g * c, (g + 1) * c)
            p_re = jnp.dot(y[:FFT_SLAB, cols], wf_ref[g, :c, :], preferred_element_type=F32)
            p_im = jnp.dot(y[FFT_SLAB:, cols], wf_ref[g, c:, :], preferred_element_type=F32)
            fwd = (p_re + p_im).astype(BF16)
            for k3 in range(FFT_SLABS):
                ym_ref[k3, k1 * FFT_SLABS:(k1 + 1) * FFT_SLABS, cols] = (
                    fwd[k3 * FFT_SLABS:(k3 + 1) * FFT_SLABS])
            if k1 % half != 0:
                mir = (p_re - p_im).astype(BF16)
                km = FFT_SLABS - k1
                for k3 in range(FFT_SLABS):
                    src = FFT_SLABS - 1 - k3
                    ym_ref[k3, km * FFT_SLABS:(km + 1) * FFT_SLABS, cols] = (
                        mir[src * FFT_SLABS:(src + 1) * FFT_SLABS])

    for k3 in range(FFT_SLABS):
        rows = slice(k3 * FFT_SLAB, (k3 + 1) * FFT_SLAB)
        z = jnp.dot(p_ref[...], ym_ref[k3], preferred_element_type=F32)
        o_ref[rows, :] = (z * sg_ref[rows, :].astype(F32)).astype(o_ref.dtype)


def fftmix(xb, wf, sg, bsz, tc=512):
    e = xb.shape[1]
    gpt = tc // B_GROUP_DIM
    tile = pl.BlockSpec((None, SEQ, tc), lambda b, j: (b, 0, j))
    once = pl.Buffered(1)
    out = pl.pallas_call(
        functools.partial(_fftmix_kernel, tc=tc),
        grid=(bsz, e // tc),
        in_specs=[tile,
                  pl.BlockSpec((FFT_SLAB, FFT_SLAB), lambda b, j: (0, 0), pipeline_mode=once),
                  pl.BlockSpec((FFT_SLABS // 2 + 1, 2 * FFT_SLAB, 2 * FFT_SLAB), lambda b, j: (0, 0, 0),
                               pipeline_mode=once),
                  pl.BlockSpec((gpt, 2 * B_GROUP_DIM, B_GROUP_DIM), lambda b, j: (j, 0, 0)),
                  pl.BlockSpec((FFT_SLAB, FFT_SLAB), lambda b, j: (0, 0), pipeline_mode=once),
                  tile],
        out_specs=tile,
        out_shape=jax.ShapeDtypeStruct((bsz, SEQ, e), BF16),
        scratch_shapes=[pltpu.VMEM((FFT_SLABS // 2 + 1, 2 * FFT_SLAB, tc), BF16),
                        pltpu.VMEM((FFT_SLABS, FFT_SLAB, tc), BF16)],
        compiler_params=_params("parallel", "parallel"),
        name="fftmix",
    )(xb.reshape(bsz, SEQ, e), _stage1_table(), _slab_dft_tables(), wf, _interleave_table(),
      sg.reshape(bsz, SEQ, e))
    return out.reshape(bsz * SEQ, e)


POOL_ROWS = 128
POOL_K = 256


def _cmix_kernel(prev_ref, cur_ref, next_ref, h_ref, wg_ref, wm_ref, scale_ref, o_ref,
                 wgbf_ref, wmbf_ref, *, tm):
    g = pl.program_id(0)
    i = pl.program_id(1)

    @pl.when(i == 0)
    def _():
        wgbf_ref[...] = wg_ref[...].astype(BF16)
        wmbf_ref[...] = wm_ref[...].astype(BF16)

    win = jnp.left_shift(2, g)
    lo_off = win // 2
    hi_off = win - 1 - lo_off
    t0 = (i % (SEQ // tm)) * tm
    ext = jnp.concatenate([prev_ref[...].astype(BF16), cur_ref[...].astype(BF16),
                           next_ref[...].astype(BF16)], axis=0)
    sg = _act(jnp.dot(h_ref[...], wgbf_ref[...], preferred_element_type=F32), "silu")
    pooled = []
    for j in range(tm // POOL_ROWS):
        start = min(j * POOL_ROWS, tm + 2 * HALO - POOL_K)
        t = t0 + j * POOL_ROWS + lax.broadcasted_iota(jnp.int32, (POOL_ROWS, 1), 0)
        u = t0 + (start - HALO) + lax.broadcasted_iota(jnp.int32, (1, POOL_K), 1)
        lo = jnp.maximum(t - lo_off, 0)
        hi = jnp.minimum(t + hi_off, SEQ - 1)
        band = jnp.where((u >= lo) & (u <= hi), 1.0, 0.0).astype(BF16)
        wsum = jnp.dot(band, ext[start:start + POOL_K], preferred_element_type=F32)
        cnt = (hi - lo + 1).astype(F32)
        rows = slice(j * POOL_ROWS, (j + 1) * POOL_ROWS)
        pooled.append((wsum / cnt - cur_ref[rows, :]).astype(BF16))
    p = jnp.concatenate(pooled, axis=0)
    acc = jnp.dot(p, wmbf_ref[...], preferred_element_type=F32)
    o_ref[...] = (acc * scale_ref[...] * sg).astype(o_ref.dtype)


def cmix(xc, h, w_in, w_mix, layer, gate_col0, scale, tm=1024):
    m, e = xc.shape
    k = h.shape[1]
    c = C_GROUP_DIM
    gb0 = gate_col0 // c
    hb = tm // HALO
    last_hb = m // HALO - 1
    return pl.pallas_call(
        functools.partial(_cmix_kernel, tm=tm),
        grid=(C_GROUPS, m // tm),
        in_specs=[pl.BlockSpec((HALO, c), lambda g, i: (jnp.maximum(i * hb - 1, 0), g)),
                  pl.BlockSpec((tm, c), lambda g, i: (i, g)),
                  pl.BlockSpec((HALO, c), lambda g, i: (jnp.minimum((i + 1) * hb, last_hb), g)),
                  pl.BlockSpec((tm, k), lambda g, i: (i, 0)),
                  pl.BlockSpec((None, k, c), lambda g, i: (layer, 0, gb0 + g),
                               pipeline_mode=pl.Buffered(1)),
                  pl.BlockSpec((None, None, c, c), lambda g, i: (layer, g, 0, 0),
                               pipeline_mode=pl.Buffered(1)),
                  pl.BlockSpec((1, c), lambda g, i: (0, g))],
        out_specs=pl.BlockSpec((tm, c), lambda g, i: (i, g)),
        out_shape=jax.ShapeDtypeStruct((m, e), BF16),
        scratch_shapes=[pltpu.VMEM((k, c), BF16), pltpu.VMEM((c, c), BF16)],
        compiler_params=_params("arbitrary", "arbitrary"),
        name="cmix",
    )(xc, xc, xc, h, w_in, w_mix, scale.reshape(1, e))


def kernel(x, a_norm, a_w_in, a_v_gain, a_w_s, a_b_s, a_w_out, b_norm, b_w_in, b_w_mix, b_w_out,
           c_norm, c_w_in, c_w_mix, c_scale, c_w_out, final_norm):
    bsz, seq, d = x.shape
    assert (seq, d) == (SEQ, D_MODEL)
    e = D_INNER
    depth = 4
    norms = {0: a_norm, 1: b_norm, 2: c_norm}

    def gain_for(i):
        if i == depth:
            return final_norm
        return norms[i % 3][i // 3]

    xf = x.reshape(bsz * seq, d)
    h = rms_cast(xf, gain_for(0))
    for i in range(depth):
        kind, j = i % 3, i // 3
        if kind == 0:
            u, w_out = proj(h, a_w_in, j, 0, e, None, BF16, cast=(a_w_out, j))
            gv, ssq = proj(h, a_w_in, j, e, e, "gelu", BF16, with_ssq=True)
            y = amix(h, a_w_in, j, 2 * e, u, gv, ssq, a_v_gain[j], a_w_s[j], a_b_s[j])
        elif kind == 1:
            xb, w_out = proj(h, b_w_in, j, 0, e, None, BF16, cast=(b_w_out, j))
            (sg,) = proj(h, b_w_in, j, e, e, "silu", BF16)
            wf = fold_channel_dft(b_w_mix, j)
            y = fftmix(xb, wf, sg, bsz)
        else:
            xc, w_out = proj(h, c_w_in, j, 0, e, None, F32, cast=(c_w_out, j))
            y = cmix(xc, h, c_w_in, c_w_mix, j, e, c_scale[j])
        last = i == depth - 1
        res = out_proj(y, w_out, xf, gain_for(i + 1), last)
        if last:
            return res.reshape(bsz, seq, d)
        xf, h = res
```

```python
import functools

import numpy as np
import jax
import jax.numpy as jnp
from jax import lax
from jax.experimental import pallas as pl
from jax.experimental.pallas import tpu as pltpu

D_MODEL = 2048
D_INNER = 4096
SEQ = 4096
CHUNK = 128
A_GROUPS = 8
A_GROUP_DIM = D_INNER // A_GROUPS
B_GROUPS = 8
B_GROUP_DIM = D_INNER // B_GROUPS
POOL_WINDOWS = (2, 4, 8, 16)
C_GROUPS = len(POOL_WINDOWS)
C_GROUP_DIM = D_INNER // C_GROUPS
EPS = 1e-6
HALO = 16

FFT_SLABS = 16
FFT_SLAB = SEQ // FFT_SLABS
BF16_ROWS = 16

VMEM_LIMIT = 56 * 1024 * 1024

F32 = jnp.float32
BF16 = jnp.bfloat16


def _params(*sem):
    return pltpu.CompilerParams(dimension_semantics=sem, vmem_limit_bytes=VMEM_LIMIT)


def _norm_kernel(x_ref, g_ref, h_ref):
    x = x_ref[...]
    r = lax.rsqrt(jnp.mean(x * x, axis=-1, keepdims=True) + EPS)
    h_ref[...] = (x * r * g_ref[...]).astype(h_ref.dtype)


def rms_cast(x, g, tm=512):
    m, d = x.shape
    return pl.pallas_call(
        _norm_kernel,
        grid=(m // tm,),
        in_specs=[pl.BlockSpec((tm, d), lambda i: (i, 0)),
                  pl.BlockSpec((1, d), lambda i: (0, 0))],
        out_specs=pl.BlockSpec((tm, d), lambda i: (i, 0)),
        out_shape=jax.ShapeDtypeStruct((m, d), BF16),
        compiler_params=_params("parallel"),
        name="rms_cast",
    )(x, g.reshape(1, d))


GELU_C = 0.7978845608028654
GELU_A = 0.044715


def _act(x, act):
    if act is None:
        return x
    hx = 0.5 * x
    if act == "gelu":
        t = jnp.tanh(x * (GELU_C + (GELU_C * GELU_A) * (x * x)))
    else:
        t = jnp.tanh(hx)
    return hx + hx * t


def _proj_kernel(a_ref, w_ref, *refs, act, with_ssq, with_cast):
    refs = list(refs)
    cast_in = refs.pop(0) if with_cast else None
    o_ref = refs.pop(0)
    ss_ref = refs.pop(0) if with_ssq else None
    cast_out = refs.pop(0) if with_cast else None
    (wbf_ref,) = refs

    @pl.when(pl.program_id(1) == 0)
    def _():
        wbf_ref[...] = w_ref[...].astype(BF16)

    val = _act(jnp.dot(a_ref[...], wbf_ref[...], preferred_element_type=F32), act)
    o_ref[...] = val.astype(o_ref.dtype)
    if with_ssq:
        ss_ref[...] = jnp.sum(val * val, axis=-1, keepdims=True)
    if with_cast:
        cast_out[...] = cast_in[...].astype(cast_out.dtype)


def proj(a, w, layer, col0, ncols, act, out_dtype, with_ssq=False, cast=None, tm=1024, tn=1024):
    m, k = a.shape
    nb0 = col0 // tn
    n_tiles = ncols // tn
    n_m = m // tm
    in_specs = [pl.BlockSpec((tm, k), lambda j, i: (i, 0)),
                pl.BlockSpec((None, k, tn), lambda j, i: (layer, 0, nb0 + j))]
    args = [a, w]
    out_specs = [pl.BlockSpec((tm, tn), lambda j, i: (i, j))]
    out_shape = [jax.ShapeDtypeStruct((m, ncols), out_dtype)]
    if with_ssq:
        out_specs.append(pl.BlockSpec((None, tm, 1), lambda j, i: (j, i, 0)))
        out_shape.append(jax.ShapeDtypeStruct((n_tiles, m, 1), F32))
    if cast is not None:
        cw, cl = cast
        _, ce, cd = cw.shape
        slab = ce // (n_tiles * n_m)
        assert slab * n_tiles * n_m == ce and slab % BF16_ROWS == 0
        in_specs.append(pl.BlockSpec((None, slab, cd), lambda j, i: (cl, j * n_m + i, 0)))
        args.append(cw)
        out_specs.append(pl.BlockSpec((slab, cd), lambda j, i: (j * n_m + i, 0)))
        out_shape.append(jax.ShapeDtypeStruct((ce, cd), BF16))
    return pl.pallas_call(
        functools.partial(_proj_kernel, act=act, with_ssq=with_ssq, with_cast=cast is not None),
        grid=(n_tiles, n_m),
        in_specs=in_specs,
        out_specs=out_specs,
        out_shape=out_shape,
        scratch_shapes=[pltpu.VMEM((k, tn), BF16)],
        compiler_params=_params("arbitrary", "arbitrary"),
        name="proj_" + str(act),
    )(*args)


def _out_kernel(y_ref, w_ref, x_ref, g_ref, *refs, n_tiles, tn, last):
    if last:
        o_ref, xrow = refs
    else:
        xo_ref, o_ref, xrow = refs
    n = pl.program_id(1)
    xn = x_ref[...] + jnp.dot(y_ref[...], w_ref[...], preferred_element_type=F32)
    if not last:
        xo_ref[...] = xn
    xrow[n] = xn

    @pl.when(n == n_tiles - 1)
    def _():
        ss = None
        for j in range(n_tiles):
            v = xrow[j]
            s = jnp.sum(v * v, axis=-1, keepdims=True)
            ss = s if ss is None else ss + s
        r = lax.rsqrt(ss * (1.0 / (n_tiles * tn)) + EPS)
        for j in range(n_tiles):
            cols = slice(j * tn, (j + 1) * tn)
            o_ref[:, cols] = (xrow[j] * r * g_ref[:, cols]).astype(o_ref.dtype)


def out_proj(y, w, x, g, last, tm=1024, tn=512):
    m, k = y.shape
    d = w.shape[1]
    n_tiles = d // tn
    in_specs = [pl.BlockSpec((tm, k), lambda i, n: (i, 0)),
                pl.BlockSpec((k, tn), lambda i, n: (0, n)),
                pl.BlockSpec((tm, tn), lambda i, n: (i, n)),
                pl.BlockSpec((1, d), lambda i, n: (0, 0))]
    row_spec = pl.BlockSpec((tm, d), lambda i, n: (i, 0))
    if last:
        out_specs = row_spec
        out_shape = jax.ShapeDtypeStruct((m, d), F32)
    else:
        out_specs = [pl.BlockSpec((tm, tn), lambda i, n: (i, n)), row_spec]
        out_shape = [jax.ShapeDtypeStruct((m, d), F32), jax.ShapeDtypeStruct((m, d), BF16)]
    return pl.pallas_call(
        functools.partial(_out_kernel, n_tiles=n_tiles, tn=tn, last=last),
        grid=(m // tm, n_tiles),
        in_specs=in_specs,
        out_specs=out_specs,
        out_shape=out_shape,
        scratch_shapes=[pltpu.VMEM((n_tiles, tm, tn), F32)],
        compiler_params=_params("parallel", "arbitrary"),
        name="out_proj",
    )(y, w, x, g.reshape(1, d))


AMIX_GROUPS_PER_STEP = 2


def _amix_kernel(h_ref, w_ref, u_ref, gv_ref, ss_ref, gain_ref, ws_ref, b_ref, o_ref, wbf_ref, t_ref,
                 *, tm):
    i, s = pl.program_id(0), pl.program_id(1)
    gps, c = AMIX_GROUPS_PER_STEP, A_GROUP_DIM

    @pl.when(i == 0)
    def _():
        for gg in range(gps):
            wbf_ref[s * gps + gg] = w_ref[:, gg * c:(gg + 1) * c].astype(BF16)

    ssq = ss_ref[0]
    for t in range(1, ss_ref.shape[0]):
        ssq = ssq + ss_ref[t]
    r = lax.rsqrt(ssq * (1.0 / D_INNER) + EPS)
    for gg in range(gps):
        cols = slice(gg * c, (gg + 1) * c)
        vn = (gv_ref[:, cols].astype(F32) * r * gain_ref[:, cols]).astype(BF16)
        wg = ws_ref[gg]
        bias = b_ref[gg]
        for ch in range(tm // CHUNK):
            rows = slice(ch * CHUNK, (ch + 1) * CHUNK)
            sv = jnp.dot(wg, vn[rows], preferred_element_type=F32) + bias
            t_ref[rows, cols] = _act(u_ref[rows, cols].astype(F32), "gelu") * sv
        sg = _act(jnp.dot(h_ref[...], wbf_ref[s * gps + gg], preferred_element_type=F32), "silu")
        o_ref[:, cols] = (t_ref[:, cols] * sg).astype(o_ref.dtype)


def amix(h, w_in, layer, gate_col0, u, gv, ssq, gain, w_s, b_s, tm=1024):
    m, e = u.shape
    k = h.shape[1]
    gps = AMIX_GROUPS_PER_STEP
    c = gps * A_GROUP_DIM
    n_steps = A_GROUPS // gps
    gb0 = gate_col0 // c
    n_ss = ssq.shape[0]
    grp = pl.BlockSpec((tm, c), lambda i, s: (i, s))
    return pl.pallas_call(
        functools.partial(_amix_kernel, tm=tm),
        grid=(m // tm, n_steps),
        in_specs=[pl.BlockSpec((tm, k), lambda i, s: (i, 0)),
                  pl.BlockSpec((None, k, c),
                               lambda i, s: (layer, 0, gb0 + jnp.where(i == 0, s, n_steps - 1)),
                               pipeline_mode=pl.Buffered(1)),
                  grp, grp,
                  pl.BlockSpec((n_ss, tm, 1), lambda i, s: (0, i, 0)),
                  pl.BlockSpec((1, c), lambda i, s: (0, s)),
                  pl.BlockSpec((gps, CHUNK, CHUNK), lambda i, s: (s, 0, 0)),
                  pl.BlockSpec((gps, CHUNK, 1), lambda i, s: (s, 0, 0))],
        out_specs=grp,
        out_shape=jax.ShapeDtypeStruct((m, e), BF16),
        scratch_shapes=[pltpu.VMEM((A_GROUPS, k, A_GROUP_DIM), BF16), pltpu.VMEM((tm, c), F32)],
        compiler_params=_params("arbitrary", "arbitrary"),
        name="amix",
    )(h, w_in, u, gv, ssq, gain.reshape(1, e), w_s.astype(BF16), b_s.reshape(A_GROUPS, CHUNK, 1))


def _dft_tables(n):
    idx = jnp.arange(n, dtype=jnp.int32)
    ang = ((idx[:, None] * idx[None, :]) % n).astype(F32) * (2.0 * jnp.pi / n)
    return jnp.cos(ang), jnp.sin(ang)


def _fold_kernel(cs_ref, w_ref, o_ref, *, scale):
    acc = jnp.dot(cs_ref[...], w_ref[...].astype(BF16), preferred_element_type=F32)
    o_ref[...] = (acc * scale).astype(o_ref.dtype)


def fold_channel_dft(w_mix, layer):
    _, g, c, _ = w_mix.shape
    cc, sc = _dft_tables(c)
    cs = jnp.concatenate([cc, sc], axis=0).astype(BF16)
    scale = float((SEQ * c) ** -0.5)
    return pl.pallas_call(
        functools.partial(_fold_kernel, scale=scale),
        grid=(g,),
        in_specs=[pl.BlockSpec((2 * c, c), lambda i: (0, 0)),
                  pl.BlockSpec((None, None, c, c), lambda i: (layer, i, 0, 0))],
        out_specs=pl.BlockSpec((None, 2 * c, c), lambda i: (i, 0, 0)),
        out_shape=jax.ShapeDtypeStruct((g, 2 * c, c), BF16),
        compiler_params=_params("parallel"),
        name="fold_channel_dft",
    )(cs, w_mix)


def _slab_dft_tables():
    k1 = jnp.arange(FFT_SLABS // 2 + 1, dtype=jnp.int32)[:, None, None]
    j = jnp.arange(FFT_SLAB, dtype=jnp.int32)[None, :, None]
    n2 = jnp.arange(FFT_SLAB, dtype=jnp.int32)[None, None, :]
    a1 = ((k1 * n2) % SEQ).astype(F32) * (2.0 * jnp.pi / SEQ)
    a2 = ((j * n2) % FFT_SLAB).astype(F32) * (2.0 * jnp.pi / FFT_SLAB)
    c1, s1, c2, s2 = jnp.cos(a1), jnp.sin(a1), jnp.cos(a2), jnp.sin(a2)
    cos, sin = c1 * c2 - s1 * s2, s1 * c2 + c1 * s2
    top = jnp.concatenate([cos, sin], axis=2)
    bot = jnp.concatenate([-sin, cos], axis=2)
    return jnp.concatenate([top, bot], axis=1).astype(BF16)


def _interleave_table():
    k1, r = np.divmod(np.arange(FFT_SLAB), FFT_SLABS)
    k2 = np.where(k1 <= FFT_SLABS // 2, r, FFT_SLABS - 1 - r)
    p = np.zeros((FFT_SLAB, FFT_SLAB), np.float32)
    p[FFT_SLABS * k2 + k1, np.arange(FFT_SLAB)] = 1.0
    return jnp.asarray(p, dtype=BF16)


_STAGE1_BLOCKS = ([(0, 0)] + [(k1, part) for k1 in range(1, FFT_SLABS // 2) for part in (0, 1)]
                  + [(FFT_SLABS // 2, 0)])


def _stage1_table():
    k1 = jnp.asarray([b[0] for b in _STAGE1_BLOCKS], jnp.int32)[:, None]
    is_im = jnp.asarray([b[1] for b in _STAGE1_BLOCKS], jnp.int32)[:, None]
    m = jnp.arange(FFT_SLABS, dtype=jnp.int32)[None, :]
    ang = ((k1 * m) % FFT_SLABS).astype(F32) * (2.0 * jnp.pi / FFT_SLABS)
    coef = jnp.where(is_im == 1, -jnp.sin(ang), jnp.cos(ang))
    return jnp.kron(coef, jnp.eye(BF16_ROWS, dtype=F32)).astype(BF16)


def _fftmix_kernel(x_ref, m1_ref, g_ref, wf_ref, p_ref, sg_ref, o_ref, a_ref, ym_ref, *, tc):
    half = FFT_SLABS // 2
    c = B_GROUP_DIM

    for grp in range(FFT_SLAB // BF16_ROWS):
        r0 = grp * BF16_ROWS
        xg = jnp.concatenate([x_ref[m * FFT_SLAB + r0:m * FFT_SLAB + r0 + BF16_ROWS, :]
                              for m in range(FFT_SLABS)], axis=0)
        s1 = jnp.dot(m1_ref[...], xg, preferred_element_type=F32).astype(BF16)
        for blk, (k1, part) in enumerate(_STAGE1_BLOCKS):
            a_ref[k1, part * FFT_SLAB + r0:part * FFT_SLAB + r0 + BF16_ROWS, :] = (
                s1[blk * BF16_ROWS:(blk + 1) * BF16_ROWS])

    for k1 in range(half + 1):
        if k1 % half == 0:
            y = jnp.dot(g_ref[k1, :, :FFT_SLAB], a_ref[k1, :FFT_SLAB, :], preferred_element_type=F32)
        else:
            y = jnp.dot(g_ref[k1], a_ref[k1], preferred_element_type=F32)
        y = y.astype(BF16)
        for g in range(tc // c):
            cols = slice(g * c, (g + 1) * c)
            p_re = jnp.dot(y[:FFT_SLAB, cols], wf_ref[g, :c, :], preferred_element_type=F32)
            p_im = jnp.dot(y[FFT_SLAB:, cols], wf_ref[g, c:, :], preferred_element_type=F32)
            fwd = (p_re + p_im).astype(BF16)
            for k3 in range(FFT_SLABS):
                ym_ref[k3, k1 * FFT_SLABS:(k1 + 1) * FFT_SLABS, cols] = (
                    fwd[k3 * FFT_SLABS:(k3 + 1) * FFT_SLABS])
            if k1 % half != 0:
                mir = (p_re - p_im).astype(BF16)
                km = FFT_SLABS - k1
                for k3 in range(FFT_SLABS):
                    src = FFT_SLABS - 1 - k3
                    ym_ref[k3, km * FFT_SLABS:(km + 1) * FFT_SLABS, cols] = (
                        mir[src * FFT_SLABS:(src + 1) * FFT_SLABS])

    for k3 in range(FFT_SLABS):
        rows = slice(k3 * FFT_SLAB, (k3 + 1) * FFT_SLAB)
        z = jnp.dot(p_ref[...], ym_ref[k3], preferred_element_type=F32)
        o_ref[rows, :] = (z * sg_ref[rows, :].astype(F32)).astype(o_ref.dtype)


def fftmix(xb, wf, sg, bsz, tc=512):
    e = xb.shape[1]
    gpt = tc // B_GROUP_DIM
    tile = pl.BlockSpec((None, SEQ, tc), lambda b, j: (b, 0, j))
    once = pl.Buffered(1)
    out = pl.pallas_call(
        functools.partial(_fftmix_kernel, tc=tc),
        grid=(bsz, e // tc),
        in_specs=[tile,
                  pl.BlockSpec((FFT_SLAB, FFT_SLAB), lambda b, j: (0, 0), pipeline_mode=once),
                  pl.BlockSpec((FFT_SLABS // 2 + 1, 2 * FFT_SLAB, 2 * FFT_SLAB), lambda b, j: (0, 0, 0),
                               pipeline_mode=once),
                  pl.BlockSpec((gpt, 2 * B_GROUP_DIM, B_GROUP_DIM), lambda b, j: (j, 0, 0)),
                  pl.BlockSpec((FFT_SLAB, FFT_SLAB), lambda b, j: (0, 0), pipeline_mode=once),
                  tile],
        out_specs=tile,
        out_shape=jax.ShapeDtypeStruct((bsz, SEQ, e), BF16),
        scratch_shapes=[pltpu.VMEM((FFT_SLABS // 2 + 1, 2 * FFT_SLAB, tc), BF16),
                        pltpu.VMEM((FFT_SLABS, FFT_SLAB, tc), BF16)],
        compiler_params=_params("parallel", "parallel"),
        name="fftmix",
    )(xb.reshape(bsz, SEQ, e), _stage1_table(), _slab_dft_tables(), wf, _interleave_table(),
      sg.reshape(bsz, SEQ, e))
    return out.reshape(bsz * SEQ, e)


POOL_ROWS = 128
POOL_K = 256
FOLD_ROWS = 512


def _cfold_kernel(wc_ref, wm_ref, cast_in, o_ref, cast_out, wmbf_ref):
    @pl.when(pl.program_id(1) == 0)
    def _():
        wmbf_ref[...] = wm_ref[...].astype(BF16)

    o_ref[...] = jnp.dot(wc_ref[...].astype(BF16), wmbf_ref[...],
                         preferred_element_type=F32).astype(o_ref.dtype)
    cast_out[...] = cast_in[...].astype(cast_out.dtype)


def fold_pool_weights(w_in, w_mix, w_out, layer):
    _, d, _ = w_in.shape
    c = C_GROUP_DIM
    n_k = d // FOLD_ROWS
    _, ce, cd = w_out.shape
    slab = ce // (C_GROUPS * n_k)
    assert slab * C_GROUPS * n_k == ce and slab % BF16_ROWS == 0
    return pl.pallas_call(
        _cfold_kernel,
        grid=(C_GROUPS, n_k),
        in_specs=[pl.BlockSpec((None, FOLD_ROWS, c), lambda g, t: (layer, t, g)),
                  pl.BlockSpec((None, None, c, c), lambda g, t: (layer, g, 0, 0)),
                  pl.BlockSpec((None, slab, cd), lambda g, t: (layer, g * n_k + t, 0))],
        out_specs=[pl.BlockSpec((None, FOLD_ROWS, c), lambda g, t: (g, t, 0)),
                   pl.BlockSpec((slab, cd), lambda g, t: (g * n_k + t, 0))],
        out_shape=[jax.ShapeDtypeStruct((C_GROUPS, d, c), BF16),
                   jax.ShapeDtypeStruct((ce, cd), BF16)],
        scratch_shapes=[pltpu.VMEM((c, c), BF16)],
        compiler_params=_params("arbitrary", "arbitrary"),
        name="fold_pool_weights",
    )(w_in, w_mix, w_out)


def _cmix_kernel(prev_ref, cur_ref, next_ref, wf_ref, wg_ref, scale_ref, o_ref, wgbf_ref, *, tm):
    g = pl.program_id(0)
    i = pl.program_id(1)

    @pl.when(i == 0)
    def _():
        wgbf_ref[...] = wg_ref[...].astype(BF16)

    win = jnp.left_shift(2, g)
    lo_off = win // 2
    hi_off = win - 1 - lo_off
    t0 = (i % (SEQ // tm)) * tm
    ext = jnp.concatenate([prev_ref[...], cur_ref[...], next_ref[...]], axis=0)
    sg = _act(jnp.dot(cur_ref[...], wgbf_ref[...], preferred_element_type=F32), "silu")
    pooled = []
    for j in range(tm // POOL_ROWS):
        start = min(j * POOL_ROWS, tm + 2 * HALO - POOL_K)
        t = t0 + j * POOL_ROWS + lax.broadcasted_iota(jnp.int32, (POOL_ROWS, 1), 0)
        u = t0 + (start - HALO) + lax.broadcasted_iota(jnp.int32, (1, POOL_K), 1)
        lo = jnp.maximum(t - lo_off, 0)
        hi = jnp.minimum(t + hi_off, SEQ - 1)
        band = jnp.where((u >= lo) & (u <= hi), 1.0, 0.0).astype(BF16)
        wsum = jnp.dot(band, ext[start:start + POOL_K], preferred_element_type=F32)
        cnt = (hi - lo + 1).astype(F32)
        rows = slice(j * POOL_ROWS, (j + 1) * POOL_ROWS)
        pooled.append((wsum / cnt - cur_ref[rows, :].astype(F32)).astype(BF16))
    hp = jnp.concatenate(pooled, axis=0)
    acc = jnp.dot(hp, wf_ref[...], preferred_element_type=F32)
    o_ref[...] = (acc * scale_ref[...] * sg).astype(o_ref.dtype)


def cmix(h, wfold, w_in, layer, gate_col0, scale, tm=1024):
    m, k = h.shape
    c = C_GROUP_DIM
    e = C_GROUPS * c
    gb0 = gate_col0 // c
    hb = tm // HALO
    last_hb = m // HALO - 1
    return pl.pallas_call(
        functools.partial(_cmix_kernel, tm=tm),
        grid=(C_GROUPS, m // tm),
        in_specs=[pl.BlockSpec((HALO, k), lambda g, i: (jnp.maximum(i * hb - 1, 0), 0)),
                  pl.BlockSpec((tm, k), lambda g, i: (i, 0)),
                  pl.BlockSpec((HALO, k), lambda g, i: (jnp.minimum((i + 1) * hb, last_hb), 0)),
                  pl.BlockSpec((None, k, c), lambda g, i: (g, 0, 0)),
                  pl.BlockSpec((None, k, c), lambda g, i: (layer, 0, gb0 + g),
                               pipeline_mode=pl.Buffered(1)),
                  pl.BlockSpec((1, c), lambda g, i: (0, g))],
        out_specs=pl.BlockSpec((tm, c), lambda g, i: (i, g)),
        out_shape=jax.ShapeDtypeStruct((m, e), BF16),
        scratch_shapes=[pltpu.VMEM((k, c), BF16)],
        compiler_params=_params("arbitrary", "arbitrary"),
        name="cmix",
    )(h, h, h, wfold, w_in, scale.reshape(1, e))


def kernel(x, a_norm, a_w_in, a_v_gain, a_w_s, a_b_s, a_w_out, b_norm, b_w_in, b_w_mix, b_w_out,
           c_norm, c_w_in, c_w_mix, c_scale, c_w_out, final_norm):
    bsz, seq, d = x.shape
    assert (seq, d) == (SEQ, D_MODEL)
    e = D_INNER
    depth = 4
    norms = {0: a_norm, 1: b_norm, 2: c_norm}

    def gain_for(i):
        if i == depth:
            return final_norm
        return norms[i % 3][i // 3]

    xf = x.reshape(bsz * seq, d)
    h = rms_cast(xf, gain_for(0))
    for i in range(depth):
        kind, j = i % 3, i // 3
        if kind == 0:
            u, w_out = proj(h, a_w_in, j, 0, e, None, BF16, cast=(a_w_out, j))
            gv, ssq = proj(h, a_w_in, j, e, e, "gelu", BF16, with_ssq=True)
            y = amix(h, a_w_in, j, 2 * e, u, gv, ssq, a_v_gain[j], a_w_s[j], a_b_s[j])
        elif kind == 1:
            xb, w_out = proj(h, b_w_in, j, 0, e, None, BF16, cast=(b_w_out, j))
            (sg,) = proj(h, b_w_in, j, e, e, "silu", BF16)
            wf = fold_channel_dft(b_w_mix, j)
            y = fftmix(xb, wf, sg, bsz)
        else:
            wfold, w_out = fold_pool_weights(c_w_in, c_w_mix, c_w_out, j)
            y = cmix(h, wfold, c_w_in, j, e, c_scale[j])
        last = i == depth - 1
        res = out_proj(y, w_out, xf, gain_for(i + 1), last)
        if last:
            return res.reshape(bsz, seq, d)
        xf, h = res
```

```python
import functools

import numpy as np
import jax
import jax.numpy as jnp
from jax import lax
from jax.experimental import pallas as pl
from jax.experimental.pallas import tpu as pltpu

D_MODEL = 2048
D_INNER = 4096
SEQ = 4096
CHUNK = 128
A_GROUPS = 8
A_GROUP_DIM = D_INNER // A_GROUPS
B_GROUPS = 8
B_GROUP_DIM = D_INNER // B_GROUPS
POOL_WINDOWS = (2, 4, 8, 16)
C_GROUPS = len(POOL_WINDOWS)
C_GROUP_DIM = D_INNER // C_GROUPS
EPS = 1e-6
HALO = 16

FFT_SLABS = 16
FFT_SLAB = SEQ // FFT_SLABS
BF16_ROWS = 16

VMEM_LIMIT = 56 * 1024 * 1024

F32 = jnp.float32
BF16 = jnp.bfloat16


def _params(*sem):
    return pltpu.CompilerParams(dimension_semantics=sem, vmem_limit_bytes=VMEM_LIMIT)


def _norm_kernel(x_ref, g_ref, h_ref):
    x = x_ref[...]
    r = lax.rsqrt(jnp.mean(x * x, axis=-1, keepdims=True) + EPS)
    h_ref[...] = (x * r * g_ref[...]).astype(h_ref.dtype)


def rms_cast(x, g, tm=512):
    m, d = x.shape
    return pl.pallas_call(
        _norm_kernel,
        grid=(m // tm,),
        in_specs=[pl.BlockSpec((tm, d), lambda i: (i, 0)),
                  pl.BlockSpec((1, d), lambda i: (0, 0))],
        out_specs=pl.BlockSpec((tm, d), lambda i: (i, 0)),
        out_shape=jax.ShapeDtypeStruct((m, d), BF16),
        compiler_params=_params("parallel"),
        name="rms_cast",
    )(x, g.reshape(1, d))


GELU_C = 0.7978845608028654
GELU_A = 0.044715


def _act(x, act):
    if act is None:
        return x
    hx = 0.5 * x
    if act == "gelu":
        t = jnp.tanh(x * (GELU_C + (GELU_C * GELU_A) * (x * x)))
    else:
        t = jnp.tanh(hx)
    return hx + hx * t


def _proj_kernel(a_ref, w_ref, *refs, act, with_ssq, with_cast):
    refs = list(refs)
    cast_in = refs.pop(0) if with_cast else None
    o_ref = refs.pop(0)
    ss_ref = refs.pop(0) if with_ssq else None
    cast_out = refs.pop(0) if with_cast else None
    (wbf_ref,) = refs

    @pl.when(pl.program_id(1) == 0)
    def _():
        wbf_ref[...] = w_ref[...].astype(BF16)

    val = _act(jnp.dot(a_ref[...], wbf_ref[...], preferred_element_type=F32), act)
    o_ref[...] = val.astype(o_ref.dtype)
    if with_ssq:
        ss_ref[...] = jnp.sum(val * val, axis=-1, keepdims=True)
    if with_cast:
        cast_out[...] = cast_in[...].astype(cast_out.dtype)


def proj(a, w, layer, col0, ncols, act, out_dtype, with_ssq=False, cast=None, tm=1024, tn=1024):
    m, k = a.shape
    nb0 = col0 // tn
    n_tiles = ncols // tn
    n_m = m // tm
    in_specs = [pl.BlockSpec((tm, k), lambda j, i: (i, 0)),
                pl.BlockSpec((None, k, tn), lambda j, i: (layer, 0, nb0 + j))]
    args = [a, w]
    out_specs = [pl.BlockSpec((tm, tn), lambda j, i: (i, j))]
    out_shape = [jax.ShapeDtypeStruct((m, ncols), out_dtype)]
    if with_ssq:
        out_specs.append(pl.BlockSpec((None, tm, 1), lambda j, i: (j, i, 0)))
        out_shape.append(jax.ShapeDtypeStruct((n_tiles, m, 1), F32))
    if cast is not None:
        cw, cl = cast
        _, ce, cd = cw.shape
        slab = ce // (n_tiles * n_m)
        assert slab * n_tiles * n_m == ce and slab % BF16_ROWS == 0
        in_specs.append(pl.BlockSpec((None, slab, cd), lambda j, i: (cl, j * n_m + i, 0)))
        args.append(cw)
        out_specs.append(pl.BlockSpec((slab, cd), lambda j, i: (j * n_m + i, 0)))
        out_shape.append(jax.ShapeDtypeStruct((ce, cd), BF16))
    return pl.pallas_call(
        functools.partial(_proj_kernel, act=act, with_ssq=with_ssq, with_cast=cast is not None),
        grid=(n_tiles, n_m),
        in_specs=in_specs,
        out_specs=out_specs,
        out_shape=out_shape,
        scratch_shapes=[pltpu.VMEM((k, tn), BF16)],
        compiler_params=_params("arbitrary", "arbitrary"),
        name="proj_" + str(act),
    )(*args)


def _out_kernel(y_ref, w_ref, x_ref, g_ref, *refs, n_tiles, tn, last):
    if last:
        o_ref, xrow = refs
    else:
        xo_ref, o_ref, xrow = refs
    n = pl.program_id(1)
    xn = x_ref[...] + jnp.dot(y_ref[...], w_ref[...], preferred_element_type=F32)
    if not last:
        xo_ref[...] = xn
    xrow[n] = xn

    @pl.when(n == n_tiles - 1)
    def _():
        ss = None
        for j in range(n_tiles):
            v = xrow[j]
            s = jnp.sum(v * v, axis=-1, keepdims=True)
            ss = s if ss is None else ss + s
        r = lax.rsqrt(ss * (1.0 / (n_tiles * tn)) + EPS)
        for j in range(n_tiles):
            cols = slice(j * tn, (j + 1) * tn)
            o_ref[:, cols] = (xrow[j] * r * g_ref[:, cols]).astype(o_ref.dtype)


def out_proj(y, w, x, g, last, tm=1024, tn=512):
    m, k = y.shape
    d = w.shape[1]
    n_tiles = d // tn
    in_specs = [pl.BlockSpec((tm, k), lambda i, n: (i, 0)),
                pl.BlockSpec((k, tn), lambda i, n: (0, n)),
                pl.BlockSpec((tm, tn), lambda i, n: (i, n)),
                pl.BlockSpec((1, d), lambda i, n: (0, 0))]
    row_spec = pl.BlockSpec((tm, d), lambda i, n: (i, 0))
    if last:
        out_specs = row_spec
        out_shape = jax.ShapeDtypeStruct((m, d), F32)
    else:
        out_specs = [pl.BlockSpec((tm, tn), lambda i, n: (i, n)), row_spec]
        out_shape = [jax.ShapeDtypeStruct((m, d), F32), jax.ShapeDtypeStruct((m, d), BF16)]
    return pl.pallas_call(
        functools.partial(_out_kernel, n_tiles=n_tiles, tn=tn, last=last),
        grid=(m // tm, n_tiles),
        in_specs=in_specs,
        out_specs=out_specs,
        out_shape=out_shape,
        scratch_shapes=[pltpu.VMEM((n_tiles, tm, tn), F32)],
        compiler_params=_params("parallel", "arbitrary"),
        name="out_proj",
    )(y, w, x, g.reshape(1, d))


AMIX_GROUPS_PER_STEP = 2


def _amix_kernel(h_ref, w_ref, u_ref, gv_ref, ss_ref, gain_ref, ws_ref, b_ref, o_ref, wbf_ref, t_ref,
                 *, tm):
    i, s = pl.program_id(0), pl.program_id(1)
    gps, c = AMIX_GROUPS_PER_STEP, A_GROUP_DIM

    @pl.when(i == 0)
    def _():
        for gg in range(gps):
            wbf_ref[s * gps + gg] = w_ref[:, gg * c:(gg + 1) * c].astype(BF16)

    ssq = ss_ref[0]
    for t in range(1, ss_ref.shape[0]):
        ssq = ssq + ss_ref[t]
    r = lax.rsqrt(ssq * (1.0 / D_INNER) + EPS)
    for gg in range(gps):
        cols = slice(gg * c, (gg + 1) * c)
        vn = (gv_ref[:, cols].astype(F32) * r * gain_ref[:, cols]).astype(BF16)
        wg = ws_ref[gg]
        bias = b_ref[gg]
        for ch in range(tm // CHUNK):
            rows = slice(ch * CHUNK, (ch + 1) * CHUNK)
            sv = jnp.dot(wg, vn[rows], preferred_element_type=F32) + bias
            t_ref[rows, cols] = _act(u_ref[rows, cols].astype(F32), "gelu") * sv
        sg = _act(jnp.dot(h_ref[...], wbf_ref[s * gps + gg], preferred_element_type=F32), "silu")
        o_ref[:, cols] = (t_ref[:, cols] * sg).astype(o_ref.dtype)


def amix(h, w_in, layer, gate_col0, u, gv, ssq, gain, w_s, b_s, tm=1024):
    m, e = u.shape
    k = h.shape[1]
    gps = AMIX_GROUPS_PER_STEP
    c = gps * A_GROUP_DIM
    n_steps = A_GROUPS // gps
    gb0 = gate_col0 // c
    n_ss = ssq.shape[0]
    grp = pl.BlockSpec((tm, c), lambda i, s: (i, s))
    return pl.pallas_call(
        functools.partial(_amix_kernel, tm=tm),
        grid=(m // tm, n_steps),
        in_specs=[pl.BlockSpec((tm, k), lambda i, s: (i, 0)),
                  pl.BlockSpec((None, k, c),
                               lambda i, s: (layer, 0, gb0 + jnp.where(i == 0, s, n_steps - 1)),
                               pipeline_mode=pl.Buffered(1)),
                  grp, grp,
                  pl.BlockSpec((n_ss, tm, 1), lambda i, s: (0, i, 0)),
                  pl.BlockSpec((1, c), lambda i, s: (0, s)),
                  pl.BlockSpec((gps, CHUNK, CHUNK), lambda i, s: (s, 0, 0)),
                  pl.BlockSpec((gps, CHUNK, 1), lambda i, s: (s, 0, 0))],
        out_specs=grp,
        out_shape=jax.ShapeDtypeStruct((m, e), BF16),
        scratch_shapes=[pltpu.VMEM((A_GROUPS, k, A_GROUP_DIM), BF16), pltpu.VMEM((tm, c), F32)],
        compiler_params=_params("arbitrary", "arbitrary"),
        name="amix",
    )(h, w_in, u, gv, ssq, gain.reshape(1, e), w_s.astype(BF16), b_s.reshape(A_GROUPS, CHUNK, 1))


def _dft_tables(n):
    idx = jnp.arange(n, dtype=jnp.int32)
    ang = ((idx[:, None] * idx[None, :]) % n).astype(F32) * (2.0 * jnp.pi / n)
    return jnp.cos(ang), jnp.sin(ang)


def _fold_kernel(cs_ref, w_ref, o_ref, *, scale):
    acc = jnp.dot(cs_ref[...], w_ref[...].astype(BF16), preferred_element_type=F32)
    o_ref[...] = (acc * scale).astype(o_ref.dtype)


def fold_channel_dft(w_mix, layer):
    _, g, c, _ = w_mix.shape
    cc, sc = _dft_tables(c)
    cs = jnp.concatenate([cc, sc], axis=0).astype(BF16)
    scale = float((SEQ * c) ** -0.5)
    return pl.pallas_call(
        functools.partial(_fold_kernel, scale=scale),
        grid=(g,),
        in_specs=[pl.BlockSpec((2 * c, c), lambda i: (0, 0)),
                  pl.BlockSpec((None, None, c, c), lambda i: (layer, i, 0, 0))],
        out_specs=pl.BlockSpec((None, 2 * c, c), lambda i: (i, 0, 0)),
        out_shape=jax.ShapeDtypeStruct((g, 2 * c, c), BF16),
        compiler_params=_params("parallel"),
        name="fold_channel_dft",
    )(cs, w_mix)


def _slab_dft_tables():
    k1 = jnp.arange(FFT_SLABS // 2 + 1, dtype=jnp.int32)[:, None, None]
    j = jnp.arange(FFT_SLAB, dtype=jnp.int32)[None, :, None]
    n2 = jnp.arange(FFT_SLAB, dtype=jnp.int32)[None, None, :]
    a1 = ((k1 * n2) % SEQ).astype(F32) * (2.0 * jnp.pi / SEQ)
    a2 = ((j * n2) % FFT_SLAB).astype(F32) * (2.0 * jnp.pi / FFT_SLAB)
    c1, s1, c2, s2 = jnp.cos(a1), jnp.sin(a1), jnp.cos(a2), jnp.sin(a2)
    cos, sin = c1 * c2 - s1 * s2, s1 * c2 + c1 * s2
    top = jnp.concatenate([cos, sin], axis=2)
    bot = jnp.concatenate([-sin, cos], axis=2)
    return jnp.concatenate([top, bot], axis=1).astype(BF16)


def _interleave_table():
    k1, r = np.divmod(np.arange(FFT_SLAB), FFT_SLABS)
    k2 = np.where(k1 <= FFT_SLABS // 2, r, FFT_SLABS - 1 - r)
    p = np.zeros((FFT_SLAB, FFT_SLAB), np.float32)
    p[FFT_SLABS * k2 + k1, np.arange(FFT_SLAB)] = 1.0
    return jnp.asarray(p, dtype=BF16)


_STAGE1_BLOCKS = ([(0, 0)] + [(k1, part) for k1 in range(1, FFT_SLABS // 2) for part in (0, 1)]
                  + [(FFT_SLABS // 2, 0)])


def _stage1_table():
    k1 = jnp.asarray([b[0] for b in _STAGE1_BLOCKS], jnp.int32)[:, None]
    is_im = jnp.asarray([b[1] for b in _STAGE1_BLOCKS], jnp.int32)[:, None]
    m = jnp.arange(FFT_SLABS, dtype=jnp.int32)[None, :]
    ang = ((k1 * m) % FFT_SLABS).astype(F32) * (2.0 * jnp.pi / FFT_SLABS)
    coef = jnp.where(is_im == 1, -jnp.sin(ang), jnp.cos(ang))
    return jnp.kron(coef, jnp.eye(BF16_ROWS, dtype=F32)).astype(BF16)


def _fftmix_kernel(x_ref, m1_ref, g_ref, wf_ref, p_ref, sg_ref, o_ref, a_ref, ym_ref, *, tc):
    half = FFT_SLABS // 2
    c = B_GROUP_DIM

    for grp in range(FFT_SLAB // BF16_ROWS):
        r0 = grp * BF16_ROWS
        xg = jnp.concatenate([x_ref[m * FFT_SLAB + r0:m * FFT_SLAB + r0 + BF16_ROWS, :]
                              for m in range(FFT_SLABS)], axis=0)
        s1 = jnp.dot(m1_ref[...], xg, preferred_element_type=F32).astype(BF16)
        for blk, (k1, part) in enumerate(_STAGE1_BLOCKS):
            a_ref[k1, part * FFT_SLAB + r0:part * FFT_SLAB + r0 + BF16_ROWS, :] = (
                s1[blk * BF16_ROWS:(blk + 1) * BF16_ROWS])

    for k1 in range(half + 1):
        if k1 % half == 0:
            y = jnp.dot(g_ref[k1, :, :FFT_SLAB], a_ref[k1, :FFT_SLAB, :], preferred_element_type=F32)
        else:
            y = jnp.dot(g_ref[k1], a_ref[k1], preferred_element_type=F32)
        y = y.astype(BF16)
        for g in range(tc // c):
            cols = slice(g * c, (g + 1) * c)
            p_re = jnp.dot(y[:FFT_SLAB, cols], wf_ref[g, :c, :], preferred_element_type=F32)
            p_im = jnp.dot(y[FFT_SLAB:, cols], wf_ref[g, c:, :], preferred_element_type=F32)
            fwd = (p_re + p_im).astype(BF16)
            for k3 in range(FFT_SLABS):
                ym_ref[k3, k1 * FFT_SLABS:(k1 + 1) * FFT_SLABS, cols] = (
                    fwd[k3 * FFT_SLABS:(k3 + 1) * FFT_SLABS])
            if k1 % half != 0:
                mir = (p_re - p_im).astype(BF16)
                km = FFT_SLABS - k1
                for k3 in range(FFT_SLABS):
                    src = FFT_SLABS - 1 - k3
                    ym_ref[k3, km * FFT_SLABS:(km + 1) * FFT_SLABS, cols] = (
                        mir[src * FFT_SLABS:(src + 1) * FFT_SLABS])

    for k3 in range(FFT_SLABS):
        rows = slice(k3 * FFT_SLAB, (k3 + 1) * FFT_SLAB)
        z = jnp.dot(p_ref[...], ym_ref[k3], preferred_element_type=F32)
        o_ref[rows, :] = (z * sg_ref[rows, :].astype(F32)).astype(o_ref.dtype)


def fftmix(xb, wf, sg, bsz, tc=512):
    e = xb.shape[1]
    gpt = tc // B_GROUP_DIM
    tile = pl.BlockSpec((None, SEQ, tc), lambda b, j: (b, 0, j))
    once = pl.Buffered(1)
    out = pl.pallas_call(
        functools.partial(_fftmix_kernel, tc=tc),
        grid=(bsz, e // tc),
        in_specs=[tile,
                  pl.BlockSpec((FFT_SLAB, FFT_SLAB), lambda b, j: (0, 0), pipeline_mode=once),
                  pl.BlockSpec((FFT_SLABS // 2 + 1, 2 * FFT_SLAB, 2 * FFT_SLAB), lambda b, j: (0, 0, 0),
                               pipeline_mode=once),
                  pl.BlockSpec((gpt, 2 * B_GROUP_DIM, B_GROUP_DIM), lambda b, j: (j, 0, 0)),
                  pl.BlockSpec((FFT_SLAB, FFT_SLAB), lambda b, j: (0, 0), pipeline_mode=once),
                  tile],
        out_specs=tile,
        out_shape=jax.ShapeDtypeStruct((bsz, SEQ, e), BF16),
        scratch_shapes=[pltpu.VMEM((FFT_SLABS // 2 + 1, 2 * FFT_SLAB, tc), BF16),
                        pltpu.VMEM((FFT_SLABS, FFT_SLAB, tc), BF16)],
        compiler_params=_params("parallel", "parallel"),
        name="fftmix",
    )(xb.reshape(bsz, SEQ, e), _stage1_table(), _slab_dft_tables(), wf, _interleave_table(),
      sg.reshape(bsz, SEQ, e))
    return out.reshape(bsz * SEQ, e)


POOL_ROWS = 128
POOL_K = 256
FOLD_ROWS = 1024


def _cfold_kernel(wc_ref, wm_ref, o_ref, wmbf_ref):
    @pl.when(pl.program_id(1) == 0)
    def _():
        wmbf_ref[...] = wm_ref[...].astype(BF16)

    o_ref[...] = jnp.dot(wc_ref[...].astype(BF16), wmbf_ref[...],
                         preferred_element_type=F32).astype(o_ref.dtype)


def fold_pool_weights(w_in, w_mix, layer):
    _, d, _ = w_in.shape
    c = C_GROUP_DIM
    return pl.pallas_call(
        _cfold_kernel,
        grid=(C_GROUPS, d // FOLD_ROWS),
        in_specs=[pl.BlockSpec((None, FOLD_ROWS, c), lambda g, t: (layer, t, g)),
                  pl.BlockSpec((None, None, c, c), lambda g, t: (layer, g, 0, 0))],
        out_specs=pl.BlockSpec((None, FOLD_ROWS, c), lambda g, t: (g, t, 0)),
        out_shape=jax.ShapeDtypeStruct((C_GROUPS, d, c), BF16),
        scratch_shapes=[pltpu.VMEM((c, c), BF16)],
        compiler_params=_params("arbitrary", "arbitrary"),
        name="fold_pool_weights",
    )(w_in, w_mix)


def _cmix_kernel(prev_ref, cur_ref, next_ref, wf_ref, wg_ref, scale_ref, cast_in, o_ref, cast_out,
                 wgbf_ref, *, tm):
    g = pl.program_id(0)
    i = pl.program_id(1)
    cast_out[...] = cast_in[...].astype(cast_out.dtype)

    @pl.when(i == 0)
    def _():
        wgbf_ref[...] = wg_ref[...].astype(BF16)

    win = jnp.left_shift(2, g)
    lo_off = win // 2
    hi_off = win - 1 - lo_off
    t0 = (i % (SEQ // tm)) * tm
    ext = jnp.concatenate([prev_ref[...], cur_ref[...], next_ref[...]], axis=0)
    sg = _act(jnp.dot(cur_ref[...], wgbf_ref[...], preferred_element_type=F32), "silu")
    pooled = []
    for j in range(tm // POOL_ROWS):
        start = min(j * POOL_ROWS, tm + 2 * HALO - POOL_K)
        t = t0 + j * POOL_ROWS + lax.broadcasted_iota(jnp.int32, (POOL_ROWS, 1), 0)
        u = t0 + (start - HALO) + lax.broadcasted_iota(jnp.int32, (1, POOL_K), 1)
        lo = jnp.maximum(t - lo_off, 0)
        hi = jnp.minimum(t + hi_off, SEQ - 1)
        band = jnp.where((u >= lo) & (u <= hi), 1.0, 0.0).astype(BF16)
        wsum = jnp.dot(band, ext[start:start + POOL_K], preferred_element_type=F32)
        cnt = (hi - lo + 1).astype(F32)
        rows = slice(j * POOL_ROWS, (j + 1) * POOL_ROWS)
        pooled.append((wsum / cnt - cur_ref[rows, :].astype(F32)).astype(BF16))
    hp = jnp.concatenate(pooled, axis=0)
    acc = jnp.dot(hp, wf_ref[...], preferred_element_type=F32)
    o_ref[...] = (acc * scale_ref[...] * sg).astype(o_ref.dtype)


def cmix(h, wfold, w_in, layer, gate_col0, scale, w_out, tm=1024):
    m, k = h.shape
    c = C_GROUP_DIM
    e = C_GROUPS * c
    gb0 = gate_col0 // c
    hb = tm // HALO
    last_hb = m // HALO - 1
    n_m = m // tm
    _, ce, cd = w_out.shape
    slab = ce // (C_GROUPS * n_m)
    assert slab * C_GROUPS * n_m == ce and slab % BF16_ROWS == 0
    return pl.pallas_call(
        functools.partial(_cmix_kernel, tm=tm),
        grid=(C_GROUPS, m // tm),
        in_specs=[pl.BlockSpec((HALO, k), lambda g, i: (jnp.maximum(i * hb - 1, 0), 0)),
                  pl.BlockSpec((tm, k), lambda g, i: (i, 0)),
                  pl.BlockSpec((HALO, k), lambda g, i: (jnp.minimum((i + 1) * hb, last_hb), 0)),
                  pl.BlockSpec((None, k, c), lambda g, i: (g, 0, 0)),
                  pl.BlockSpec((None, k, c), lambda g, i: (layer, 0, gb0 + g),
                               pipeline_mode=pl.Buffered(1)),
                  pl.BlockSpec((1, c), lambda g, i: (0, g)),
                  pl.BlockSpec((None, slab, cd), lambda g, i: (layer, g * n_m + i, 0))],
        out_specs=[pl.BlockSpec((tm, c), lambda g, i: (i, g)),
                   pl.BlockSpec((slab, cd), lambda g, i: (g * n_m + i, 0))],
        out_shape=[jax.ShapeDtypeStruct((m, e), BF16), jax.ShapeDtypeStruct((ce, cd), BF16)],
        scratch_shapes=[pltpu.VMEM((k, c), BF16)],
        compiler_params=_params("arbitrary", "arbitrary"),
        name="cmix",
    )(h, h, h, wfold, w_in, scale.reshape(1, e), w_out)


def kernel(x, a_norm, a_w_in, a_v_gain, a_w_s, a_b_s, a_w_out, b_norm, b_w_in, b_w_mix, b_w_out,
           c_norm, c_w_in, c_w_mix, c_scale, c_w_out, final_norm):
    bsz, seq, d = x.shape
    assert (seq, d) == (SEQ, D_MODEL)
    e = D_INNER
    depth = 4
    norms = {0: a_norm, 1: b_norm, 2: c_norm}

    def gain_for(i):
        if i == depth:
            return final_norm
        return norms[i % 3][i // 3]

    xf = x.reshape(bsz * seq, d)
    h = rms_cast(xf, gain_for(0))
    for i in range(depth):
        kind, j = i % 3, i // 3
        if kind == 0:
            u, w_out = proj(h, a_w_in, j, 0, e, None, BF16, cast=(a_w_out, j))
            gv, ssq = proj(h, a_w_in, j, e, e, "gelu", BF16, with_ssq=True)
            y = amix(h, a_w_in, j, 2 * e, u, gv, ssq, a_v_gain[j], a_w_s[j], a_b_s[j])
        elif kind == 1:
            xb, w_out = proj(h, b_w_in, j, 0, e, None, BF16, cast=(b_w_out, j))
            (sg,) = proj(h, b_w_in, j, e, e, "silu", BF16)
            wf = fold_channel_dft(b_w_mix, j)
            y = fftmix(xb, wf, sg, bsz)
        else:
            wfold = fold_pool_weights(c_w_in, c_w_mix, j)
            y, w_out = cmix(h, wfold, c_w_in, j, e, c_scale[j], c_w_out)
        last = i == depth - 1
        res = out_proj(y, w_out, xf, gain_for(i + 1), last)
        if last:
            return res.reshape(bsz, seq, d)
        xf, h = res
```

```python
import functools

import numpy as np
import jax
import jax.numpy as jnp
from jax import lax
from jax.experimental import pallas as pl
from jax.experimental.pallas import tpu as pltpu

D_MODEL = 2048
D_INNER = 4096
SEQ = 4096
CHUNK = 128
A_GROUPS = 8
A_GROUP_DIM = D_INNER // A_GROUPS
B_GROUPS = 8
B_GROUP_DIM = D_INNER // B_GROUPS
POOL_WINDOWS = (2, 4, 8, 16)
C_GROUPS = len(POOL_WINDOWS)
assert POOL_WINDOWS == tuple(2 << g for g in range(C_GROUPS))
C_GROUP_DIM = D_INNER // C_GROUPS
EPS = 1e-6
HALO = 16

FFT_SLABS = 16
FFT_SLAB = SEQ // FFT_SLABS
BF16_ROWS = 16

V7X_VMEM_BYTES = 64 * 1024 * 1024
VMEM_LIMIT = V7X_VMEM_BYTES - 8 * 1024 * 1024

F32 = jnp.float32
BF16 = jnp.bfloat16


def _params(*sem):
    return pltpu.CompilerParams(dimension_semantics=sem, vmem_limit_bytes=VMEM_LIMIT)


def _norm_kernel(x_ref, g_ref, h_ref):
    x = x_ref[...]
    r = lax.rsqrt(jnp.mean(x * x, axis=-1, keepdims=True) + EPS)
    h_ref[...] = (x * r * g_ref[...]).astype(h_ref.dtype)


def rms_cast(x, g, tm=1024):
    m, d = x.shape
    return pl.pallas_call(
        _norm_kernel,
        grid=(m // tm,),
        in_specs=[pl.BlockSpec((tm, d), lambda i: (i, 0)),
                  pl.BlockSpec((1, d), lambda i: (0, 0))],
        out_specs=pl.BlockSpec((tm, d), lambda i: (i, 0)),
        out_shape=jax.ShapeDtypeStruct((m, d), BF16),
        compiler_params=_params("parallel"),
        name="rms_cast",
    )(x, g.reshape(1, d))


GELU_C = 0.7978845608028654
GELU_A = 0.044715


def _act(x, act):
    if act is None:
        return x
    hx = 0.5 * x
    if act == "gelu":
        t = jnp.tanh(x * (GELU_C + (GELU_C * GELU_A) * (x * x)))
    else:
        t = jnp.tanh(hx)
    return hx + hx * t


def _proj_kernel(a_ref, w_ref, *refs, act, with_ssq, with_cast):
    refs = list(refs)
    cast_in = refs.pop(0) if with_cast else None
    o_ref = refs.pop(0)
    ss_ref = refs.pop(0) if with_ssq else None
    cast_out = refs.pop(0) if with_cast else None
    (wbf_ref,) = refs

    @pl.when(pl.program_id(1) == 0)
    def _():
        wbf_ref[...] = w_ref[...].astype(BF16)

    val = _act(jnp.dot(a_ref[...], wbf_ref[...], preferred_element_type=F32), act)
    o_ref[...] = val.astype(o_ref.dtype)
    if with_ssq:
        ss_ref[...] = jnp.sum(val * val, axis=-1, keepdims=True)
    if with_cast:
        cast_out[...] = cast_in[...].astype(cast_out.dtype)


def proj(a, w, layer, col0, ncols, act, out_dtype, with_ssq=False, cast=None, tm=1024, tn=1024):
    m, k = a.shape
    nb0 = col0 // tn
    n_tiles = ncols // tn
    n_m = m // tm
    in_specs = [pl.BlockSpec((tm, k), lambda j, i: (i, 0)),
                pl.BlockSpec((None, k, tn), lambda j, i: (layer, 0, nb0 + j))]
    args = [a, w]
    out_specs = [pl.BlockSpec((tm, tn), lambda j, i: (i, j))]
    out_shape = [jax.ShapeDtypeStruct((m, ncols), out_dtype)]
    if with_ssq:
        out_specs.append(pl.BlockSpec((None, tm, 1), lambda j, i: (j, i, 0)))
        out_shape.append(jax.ShapeDtypeStruct((n_tiles, m, 1), F32))
    if cast is not None:
        cw, cl = cast
        _, ce, cd = cw.shape
        slab = ce // (n_tiles * n_m)
        assert slab * n_tiles * n_m == ce and slab % BF16_ROWS == 0
        in_specs.append(pl.BlockSpec((None, slab, cd), lambda j, i: (cl, j * n_m + i, 0)))
        args.append(cw)
        out_specs.append(pl.BlockSpec((slab, cd), lambda j, i: (j * n_m + i, 0)))
        out_shape.append(jax.ShapeDtypeStruct((ce, cd), BF16))
    return pl.pallas_call(
        functools.partial(_proj_kernel, act=act, with_ssq=with_ssq, with_cast=cast is not None),
        grid=(n_tiles, n_m),
        in_specs=in_specs,
        out_specs=out_specs,
        out_shape=out_shape,
        scratch_shapes=[pltpu.VMEM((k, tn), BF16)],
        compiler_params=_params("arbitrary", "arbitrary"),
        name="proj_" + str(act),
    )(*args)


def _out_kernel(y_ref, w_ref, x_ref, g_ref, *refs, n_tiles, tn, last):
    if last:
        o_ref, xrow = refs
    else:
        xo_ref, o_ref, xrow = refs
    n = pl.program_id(1)
    xn = x_ref[...] + jnp.dot(y_ref[...], w_ref[...], preferred_element_type=F32)
    if not last:
        xo_ref[...] = xn
    xrow[n] = xn

    @pl.when(n == n_tiles - 1)
    def _():
        ss = None
        for j in range(n_tiles):
            v = xrow[j]
            s = jnp.sum(v * v, axis=-1, keepdims=True)
            ss = s if ss is None else ss + s
        r = lax.rsqrt(ss * (1.0 / (n_tiles * tn)) + EPS)
        for j in range(n_tiles):
            cols = slice(j * tn, (j + 1) * tn)
            o_ref[:, cols] = (xrow[j] * r * g_ref[:, cols]).astype(o_ref.dtype)


def out_proj(y, w, x, g, last, tm=1024, tn=512):
    m, k = y.shape
    d = w.shape[1]
    n_tiles = d // tn
    in_specs = [pl.BlockSpec((tm, k), lambda i, n: (i, 0)),
                pl.BlockSpec((k, tn), lambda i, n: (0, n)),
                pl.BlockSpec((tm, tn), lambda i, n: (i, n)),
                pl.BlockSpec((1, d), lambda i, n: (0, 0))]
    row_spec = pl.BlockSpec((tm, d), lambda i, n: (i, 0))
    if last:
        out_specs = row_spec
        out_shape = jax.ShapeDtypeStruct((m, d), F32)
    else:
        out_specs = [pl.BlockSpec((tm, tn), lambda i, n: (i, n)), row_spec]
        out_shape = [jax.ShapeDtypeStruct((m, d), F32), jax.ShapeDtypeStruct((m, d), BF16)]
    return pl.pallas_call(
        functools.partial(_out_kernel, n_tiles=n_tiles, tn=tn, last=last),
        grid=(m // tm, n_tiles),
        in_specs=in_specs,
        out_specs=out_specs,
        out_shape=out_shape,
        scratch_shapes=[pltpu.VMEM((n_tiles, tm, tn), F32)],
        compiler_params=_params("parallel", "arbitrary"),
        name="out_proj",
    )(y, w, x, g.reshape(1, d))


AMIX_GROUPS_PER_STEP = 2


def _amix_kernel(h_ref, w_ref, u_ref, gv_ref, ss_ref, gain_ref, ws_ref, b_ref, o_ref, wbf_ref, t_ref,
                 *, tm):
    i, s = pl.program_id(0), pl.program_id(1)
    gps, c = AMIX_GROUPS_PER_STEP, A_GROUP_DIM

    @pl.when(i == 0)
    def _():
        for gg in range(gps):
            wbf_ref[s * gps + gg] = w_ref[:, gg * c:(gg + 1) * c].astype(BF16)

    ssq = ss_ref[0]
    for t in range(1, ss_ref.shape[0]):
        ssq = ssq + ss_ref[t]
    r = lax.rsqrt(ssq * (1.0 / D_INNER) + EPS)
    for gg in range(gps):
        cols = slice(gg * c, (gg + 1) * c)
        vn = (gv_ref[:, cols].astype(F32) * r * gain_ref[:, cols]).astype(BF16)
        wg = ws_ref[gg]
        bias = b_ref[gg]
        for ch in range(tm // CHUNK):
            rows = slice(ch * CHUNK, (ch + 1) * CHUNK)
            sv = jnp.dot(wg, vn[rows], preferred_element_type=F32) + bias
            t_ref[rows, cols] = _act(u_ref[rows, cols].astype(F32), "gelu") * sv
        sg = _act(jnp.dot(h_ref[...], wbf_ref[s * gps + gg], preferred_element_type=F32), "silu")
        o_ref[:, cols] = (t_ref[:, cols] * sg).astype(o_ref.dtype)


def amix(h, w_in, layer, gate_col0, u, gv, ssq, gain, w_s, b_s, tm=1024):
    m, e = u.shape
    k = h.shape[1]
    gps = AMIX_GROUPS_PER_STEP
    c = gps * A_GROUP_DIM
    n_steps = A_GROUPS // gps
    gb0 = gate_col0 // c
    n_ss = ssq.shape[0]
    grp = pl.BlockSpec((tm, c), lambda i, s: (i, s))
    return pl.pallas_call(
        functools.partial(_amix_kernel, tm=tm),
        grid=(m // tm, n_steps),
        in_specs=[pl.BlockSpec((tm, k), lambda i, s: (i, 0)),
                  pl.BlockSpec((None, k, c),
                               lambda i, s: (layer, 0, gb0 + jnp.where(i == 0, s, n_steps - 1)),
                               pipeline_mode=pl.Buffered(1)),
                  grp, grp,
                  pl.BlockSpec((n_ss, tm, 1), lambda i, s: (0, i, 0)),
                  pl.BlockSpec((1, c), lambda i, s: (0, s)),
                  pl.BlockSpec((gps, CHUNK, CHUNK), lambda i, s: (s, 0, 0)),
                  pl.BlockSpec((gps, CHUNK, 1), lambda i, s: (s, 0, 0))],
        out_specs=grp,
        out_shape=jax.ShapeDtypeStruct((m, e), BF16),
        scratch_shapes=[pltpu.VMEM((A_GROUPS, k, A_GROUP_DIM), BF16), pltpu.VMEM((tm, c), F32)],
        compiler_params=_params("arbitrary", "arbitrary"),
        name="amix",
    )(h, w_in, u, gv, ssq, gain.reshape(1, e), w_s.astype(BF16), b_s.reshape(A_GROUPS, CHUNK, 1))


def _dft_tables(n):
    idx = jnp.arange(n, dtype=jnp.int32)
    ang = ((idx[:, None] * idx[None, :]) % n).astype(F32) * (2.0 * jnp.pi / n)
    return jnp.cos(ang), jnp.sin(ang)


def _fold_kernel(cs_ref, w_ref, o_ref, *, scale):
    acc = jnp.dot(cs_ref[...], w_ref[...].astype(BF16), preferred_element_type=F32)
    o_ref[...] = (acc * scale).astype(o_ref.dtype)


def fold_channel_dft(w_mix, layer):
    _, g, c, _ = w_mix.shape
    cc, sc = _dft_tables(c)
    cs = jnp.concatenate([cc, sc], axis=0).astype(BF16)
    scale = float((SEQ * c) ** -0.5)
    return pl.pallas_call(
        functools.partial(_fold_kernel, scale=scale),
        grid=(g,),
        in_specs=[pl.BlockSpec((2 * c, c), lambda i: (0, 0)),
                  pl.BlockSpec((None, None, c, c), lambda i: (layer, i, 0, 0))],
        out_specs=pl.BlockSpec((None, 2 * c, c), lambda i: (i, 0, 0)),
        out_shape=jax.ShapeDtypeStruct((g, 2 * c, c), BF16),
        compiler_params=_params("parallel"),
        name="fold_channel_dft",
    )(cs, w_mix)


def _slab_dft_tables():
    k1 = jnp.arange(FFT_SLABS // 2 + 1, dtype=jnp.int32)[:, None, None]
    j = jnp.arange(FFT_SLAB, dtype=jnp.int32)[None, :, None]
    n2 = jnp.arange(FFT_SLAB, dtype=jnp.int32)[None, None, :]
    a1 = ((k1 * n2) % SEQ).astype(F32) * (2.0 * jnp.pi / SEQ)
    a2 = ((j * n2) % FFT_SLAB).astype(F32) * (2.0 * jnp.pi / FFT_SLAB)
    c1, s1, c2, s2 = jnp.cos(a1), jnp.sin(a1), jnp.cos(a2), jnp.sin(a2)
    cos, sin = c1 * c2 - s1 * s2, s1 * c2 + c1 * s2
    top = jnp.concatenate([cos, sin], axis=2)
    bot = jnp.concatenate([-sin, cos], axis=2)
    return jnp.concatenate([top, bot], axis=1).astype(BF16)


def _interleave_table():
    k1, r = np.divmod(np.arange(FFT_SLAB), FFT_SLABS)
    k2 = np.where(k1 <= FFT_SLABS // 2, r, FFT_SLABS - 1 - r)
    p = np.zeros((FFT_SLAB, FFT_SLAB), np.float32)
    p[FFT_SLABS * k2 + k1, np.arange(FFT_SLAB)] = 1.0
    return jnp.asarray(p, dtype=BF16)


_STAGE1_BLOCKS = ([(0, 0)] + [(k1, part) for k1 in range(1, FFT_SLABS // 2) for part in (0, 1)]
                  + [(FFT_SLABS // 2, 0)])


def _stage1_table():
    k1 = jnp.asarray([b[0] for b in _STAGE1_BLOCKS], jnp.int32)[:, None]
    is_im = jnp.asarray([b[1] for b in _STAGE1_BLOCKS], jnp.int32)[:, None]
    m = jnp.arange(FFT_SLABS, dtype=jnp.int32)[None, :]
    ang = ((k1 * m) % FFT_SLABS).astype(F32) * (2.0 * jnp.pi / FFT_SLABS)
    coef = jnp.where(is_im == 1, -jnp.sin(ang), jnp.cos(ang))
    return jnp.kron(coef, jnp.eye(BF16_ROWS, dtype=F32)).astype(BF16)


def _fftmix_kernel(x_ref, m1_ref, g_ref, wf_ref, p_ref, sg_ref, o_ref, a_ref, ym_ref, *, tc):
    half = FFT_SLABS // 2
    c = B_GROUP_DIM

    for grp in range(FFT_SLAB // BF16_ROWS):
        r0 = grp * BF16_ROWS
        xg = jnp.concatenate([x_ref[m * FFT_SLAB + r0:m * FFT_SLAB + r0 + BF16_ROWS, :]
                              for m in range(FFT_SLABS)], axis=0)
        s1 = jnp.dot(m1_ref[...], xg, preferred_element_type=F32).astype(BF16)
        for blk, (k1, part) in enumerate(_STAGE1_BLOCKS):
            a_ref[k1, part * FFT_SLAB + r0:part * FFT_SLAB + r0 + BF16_ROWS, :] = (
                s1[blk * BF16_ROWS:(blk + 1) * BF16_ROWS])

    for k1 in range(half + 1):
        if k1 % half == 0:
            y = jnp.dot(g_ref[k1, :, :FFT_SLAB], a_ref[k1, :FFT_SLAB, :], preferred_element_type=F32)
        else:
            y = jnp.dot(g_ref[k1], a_ref[k1], preferred_element_type=F32)
        y = y.astype(BF16)
        for g in range(tc // c):
            cols = slice(g * c, (g + 1) * c)
            p_re = jnp.dot(y[:FFT_SLAB, cols], wf_ref[g, :c, :], preferred_element_type=F32)
            p_im = jnp.dot(y[FFT_SLAB:, cols], wf_ref[g, c:, :], preferred_element_type=F32)
            fwd = (p_re + p_im).astype(BF16)
            for k3 in range(FFT_SLABS):
                ym_ref[k3, k1 * FFT_SLABS:(k1 + 1) * FFT_SLABS, cols] = (
                    fwd[k3 * FFT_SLABS:(k3 + 1) * FFT_SLABS])
            if k1 % half != 0:
                mir = (p_re - p_im).astype(BF16)
                km = FFT_SLABS - k1
                for k3 in range(FFT_SLABS):
                    src = FFT_SLABS - 1 - k3
                    ym_ref[k3, km * FFT_SLABS:(km + 1) * FFT_SLABS, cols] = (
                        mir[src * FFT_SLABS:(src + 1) * FFT_SLABS])

    for k3 in range(FFT_SLABS):
        rows = slice(k3 * FFT_SLAB, (k3 + 1) * FFT_SLAB)
        z = jnp.dot(p_ref[...], ym_ref[k3], preferred_element_type=F32)
        o_ref[rows, :] = (z * sg_ref[rows, :].astype(F32)).astype(o_ref.dtype)


def fftmix(xb, wf, sg, bsz, tc=512):
    e = xb.shape[1]
    gpt = tc // B_GROUP_DIM
    tile = pl.BlockSpec((None, SEQ, tc), lambda b, j: (b, 0, j))
    once = pl.Buffered(1)
    out = pl.pallas_call(
        functools.partial(_fftmix_kernel, tc=tc),
        grid=(bsz, e // tc),
        in_specs=[tile,
                  pl.BlockSpec((FFT_SLAB, FFT_SLAB), lambda b, j: (0, 0), pipeline_mode=once),
                  pl.BlockSpec((FFT_SLABS // 2 + 1, 2 * FFT_SLAB, 2 * FFT_SLAB), lambda b, j: (0, 0, 0),
                               pipeline_mode=once),
                  pl.BlockSpec((gpt, 2 * B_GROUP_DIM, B_GROUP_DIM), lambda b, j: (j, 0, 0)),
                  pl.BlockSpec((FFT_SLAB, FFT_SLAB), lambda b, j: (0, 0), pipeline_mode=once),
                  tile],
        out_specs=tile,
        out_shape=jax.ShapeDtypeStruct((bsz, SEQ, e), BF16),
        scratch_shapes=[pltpu.VMEM((FFT_SLABS // 2 + 1, 2 * FFT_SLAB, tc), BF16),
                        pltpu.VMEM((FFT_SLABS, FFT_SLAB, tc), BF16)],
        compiler_params=_params("parallel", "parallel"),
        name="fftmix",
    )(xb.reshape(bsz, SEQ, e), _stage1_table(), _slab_dft_tables(), wf, _interleave_table(),
      sg.reshape(bsz, SEQ, e))
    return out.reshape(bsz * SEQ, e)


POOL_ROWS = 128
POOL_K = 256
FOLD_ROWS = 1024


def _cfold_kernel(wc_ref, wm_ref, o_ref, wmbf_ref):
    @pl.when(pl.program_id(1) == 0)
    def _():
        wmbf_ref[...] = wm_ref[...].astype(BF16)

    o_ref[...] = jnp.dot(wc_ref[...].astype(BF16), wmbf_ref[...],
                         preferred_element_type=F32).astype(o_ref.dtype)


def fold_pool_weights(w_in, w_mix, layer):
    _, d, _ = w_in.shape
    c = C_GROUP_DIM
    return pl.pallas_call(
        _cfold_kernel,
        grid=(C_GROUPS, d // FOLD_ROWS),
        in_specs=[pl.BlockSpec((None, FOLD_ROWS, c), lambda g, t: (layer, t, g)),
                  pl.BlockSpec((None, None, c, c), lambda g, t: (layer, g, 0, 0))],
        out_specs=pl.BlockSpec((None, FOLD_ROWS, c), lambda g, t: (g, t, 0)),
        out_shape=jax.ShapeDtypeStruct((C_GROUPS, d, c), BF16),
        scratch_shapes=[pltpu.VMEM((c, c), BF16)],
        compiler_params=_params("arbitrary", "arbitrary"),
        name="fold_pool_weights",
    )(w_in, w_mix)


def _cmix_kernel(prev_ref, cur_ref, next_ref, wf_ref, wg_ref, scale_ref, cast_in, o_ref, cast_out,
                 wgbf_ref, *, tm):
    g = pl.program_id(0)
    i = pl.program_id(1)
    cast_out[...] = cast_in[...].astype(cast_out.dtype)

    @pl.when(i == 0)
    def _():
        wgbf_ref[...] = wg_ref[...].astype(BF16)

    win = jnp.left_shift(2, g)
    lo_off = win // 2
    hi_off = win - 1 - lo_off
    t0 = (i % (SEQ // tm)) * tm
    ext = jnp.concatenate([prev_ref[...], cur_ref[...], next_ref[...]], axis=0)
    sg = _act(jnp.dot(cur_ref[...], wgbf_ref[...], preferred_element_type=F32), "silu")
    pooled = []
    for j in range(tm // POOL_ROWS):
        start = min(j * POOL_ROWS, tm + 2 * HALO - POOL_K)
        t = t0 + j * POOL_ROWS + lax.broadcasted_iota(jnp.int32, (POOL_ROWS, 1), 0)
        u = t0 + (start - HALO) + lax.broadcasted_iota(jnp.int32, (1, POOL_K), 1)
        lo = jnp.maximum(t - lo_off, 0)
        hi = jnp.minimum(t + hi_off, SEQ - 1)
        band = jnp.where((u >= lo) & (u <= hi), 1.0, 0.0).astype(BF16)
        wsum = jnp.dot(band, ext[start:start + POOL_K], preferred_element_type=F32)
        cnt = (hi - lo + 1).astype(F32)
        rows = slice(j * POOL_ROWS, (j + 1) * POOL_ROWS)
        pooled.append((wsum / cnt - cur_ref[rows, :].astype(F32)).astype(BF16))
    hp = jnp.concatenate(pooled, axis=0)
    acc = jnp.dot(hp, wf_ref[...], preferred_element_type=F32)
    o_ref[...] = (acc * scale_ref[...] * sg).astype(o_ref.dtype)


def cmix(h, wfold, w_in, layer, gate_col0, scale, w_out, tm=1024):
    m, k = h.shape
    c = C_GROUP_DIM
    e = C_GROUPS * c
    gb0 = gate_col0 // c
    hb = tm // HALO
    last_hb = m // HALO - 1
    n_m = m // tm
    _, ce, cd = w_out.shape
    slab = ce // (C_GROUPS * n_m)
    assert slab * C_GROUPS * n_m == ce and slab % BF16_ROWS == 0
    return pl.pallas_call(
        functools.partial(_cmix_kernel, tm=tm),
        grid=(C_GROUPS, m // tm),
        in_specs=[pl.BlockSpec((HALO, k), lambda g, i: (jnp.maximum(i * hb - 1, 0), 0)),
                  pl.BlockSpec((tm, k), lambda g, i: (i, 0)),
                  pl.BlockSpec((HALO, k), lambda g, i: (jnp.minimum((i + 1) * hb, last_hb), 0)),
                  pl.BlockSpec((None, k, c), lambda g, i: (g, 0, 0)),
                  pl.BlockSpec((None, k, c), lambda g, i: (layer, 0, gb0 + g),
                               pipeline_mode=pl.Buffered(1)),
                  pl.BlockSpec((1, c), lambda g, i: (0, g)),
                  pl.BlockSpec((None, slab, cd), lambda g, i: (layer, g * n_m + i, 0))],
        out_specs=[pl.BlockSpec((tm, c), lambda g, i: (i, g)),
                   pl.BlockSpec((slab, cd), lambda g, i: (g * n_m + i, 0))],
        out_shape=[jax.ShapeDtypeStruct((m, e), BF16), jax.ShapeDtypeStruct((ce, cd), BF16)],
        scratch_shapes=[pltpu.VMEM((k, c), BF16)],
        compiler_params=_params("arbitrary", "arbitrary"),
        name="cmix",
    )(h, h, h, wfold, w_in, scale.reshape(1, e), w_out)


def kernel(x, a_norm, a_w_in, a_v_gain, a_w_s, a_b_s, a_w_out, b_norm, b_w_in, b_w_mix, b_w_out,
           c_norm, c_w_in, c_w_mix, c_scale, c_w_out, final_norm):
    bsz, seq, d = x.shape
    assert (seq, d) == (SEQ, D_MODEL)
    e = D_INNER
    depth = 4
    norms = {0: a_norm, 1: b_norm, 2: c_norm}

    def gain_for(i):
        if i == depth:
            return final_norm
        return norms[i % 3][i // 3]

    xf = x.reshape(bsz * seq, d)
    h = rms_cast(xf, gain_for(0))
    for i in range(depth):
        kind, j = i % 3, i // 3
        if kind == 0:
            u, w_out = proj(h, a_w_in, j, 0, e, None, BF16, cast=(a_w_out, j))
            gv, ssq = proj(h, a_w_in, j, e, e, "gelu", BF16, with_ssq=True)
            y = amix(h, a_w_in, j, 2 * e, u, gv, ssq, a_v_gain[j], a_w_s[j], a_b_s[j])
        elif kind == 1:
            xb, w_out = proj(h, b_w_in, j, 0, e, None, BF16, cast=(b_w_out, j))
            (sg,) = proj(h, b_w_in, j, e, e, "silu", BF16)
            wf = fold_channel_dft(b_w_mix, j)
            y = fftmix(xb, wf, sg, bsz)
        else:
            wfold = fold_pool_weights(c_w_in, c_w_mix, j)
            y, w_out = cmix(h, wfold, c_w_in, j, e, c_scale[j], c_w_out)
        last = i == depth - 1
        res = out_proj(y, w_out, xf, gain_for(i + 1), last)
        if last:
            return res.reshape(bsz, seq, d)
        xf, h = res
```

```python
import functools

import numpy as np
import jax
import jax.numpy as jnp
from jax import lax
from jax.experimental import pallas as pl
from jax.experimental.pallas import tpu as pltpu

D_MODEL = 2048
D_INNER = 4096
SEQ = 4096
CHUNK = 128
A_GROUPS = 8
A_GROUP_DIM = D_INNER // A_GROUPS
B_GROUPS = 8
B_GROUP_DIM = D_INNER // B_GROUPS
POOL_WINDOWS = (2, 4, 8, 16)
C_GROUPS = len(POOL_WINDOWS)
assert POOL_WINDOWS == tuple(2 << g for g in range(C_GROUPS))
C_GROUP_DIM = D_INNER // C_GROUPS
EPS = 1e-6
HALO = 16

FFT_SLABS = 16
FFT_SLAB = SEQ // FFT_SLABS
BF16_ROWS = 16

V7X_VMEM_BYTES = 64 * 1024 * 1024
VMEM_LIMIT = V7X_VMEM_BYTES - 8 * 1024 * 1024

F32 = jnp.float32
BF16 = jnp.bfloat16


def _params(*sem):
    return pltpu.CompilerParams(dimension_semantics=sem, vmem_limit_bytes=VMEM_LIMIT)


def _norm_kernel(x_ref, g_ref, h_ref):
    x = x_ref[...]
    r = lax.rsqrt(jnp.mean(x * x, axis=-1, keepdims=True) + EPS)
    h_ref[...] = (x * r * g_ref[...]).astype(h_ref.dtype)


def rms_cast(x, g, tm=1024):
    m, d = x.shape
    return pl.pallas_call(
        _norm_kernel,
        grid=(m // tm,),
        in_specs=[pl.BlockSpec((tm, d), lambda i: (i, 0)),
                  pl.BlockSpec((1, d), lambda i: (0, 0))],
        out_specs=pl.BlockSpec((tm, d), lambda i: (i, 0)),
        out_shape=jax.ShapeDtypeStruct((m, d), BF16),
        compiler_params=_params("parallel"),
        name="rms_cast",
    )(x, g.reshape(1, d))


GELU_C = 0.7978845608028654
GELU_A = 0.044715


def _act(x, act):
    if act is None:
        return x
    hx = 0.5 * x
    if act == "gelu":
        t = jnp.tanh(x * (GELU_C + (GELU_C * GELU_A) * (x * x)))
    else:
        t = jnp.tanh(hx)
    return hx + hx * t


def _proj_kernel(a_ref, w_ref, *refs, act, with_ssq, with_cast):
    refs = list(refs)
    cast_in = refs.pop(0) if with_cast else None
    o_ref = refs.pop(0)
    ss_ref = refs.pop(0) if with_ssq else None
    cast_out = refs.pop(0) if with_cast else None
    (wbf_ref,) = refs

    @pl.when(pl.program_id(1) == 0)
    def _():
        wbf_ref[...] = w_ref[...].astype(BF16)

    val = _act(jnp.dot(a_ref[...], wbf_ref[...], preferred_element_type=F32), act)
    o_ref[...] = val.astype(o_ref.dtype)
    if with_ssq:
        ss_ref[...] = jnp.sum(val * val, axis=-1, keepdims=True)
    if with_cast:
        cast_out[...] = cast_in[...].astype(cast_out.dtype)


def proj(a, w, layer, col0, ncols, act, out_dtype, with_ssq=False, cast=None, tm=1024, tn=1024):
    m, k = a.shape
    nb0 = col0 // tn
    n_tiles = ncols // tn
    n_m = m // tm
    in_specs = [pl.BlockSpec((tm, k), lambda j, i: (i, 0)),
                pl.BlockSpec((None, k, tn), lambda j, i: (layer, 0, nb0 + j))]
    args = [a, w]
    out_specs = [pl.BlockSpec((tm, tn), lambda j, i: (i, j))]
    out_shape = [jax.ShapeDtypeStruct((m, ncols), out_dtype)]
    if with_ssq:
        out_specs.append(pl.BlockSpec((None, tm, 1), lambda j, i: (j, i, 0)))
        out_shape.append(jax.ShapeDtypeStruct((n_tiles, m, 1), F32))
    if cast is not None:
        cw, cl = cast
        _, ce, cd = cw.shape
        slab = ce // (n_tiles * n_m)
        assert slab * n_tiles * n_m == ce and slab % BF16_ROWS == 0
        in_specs.append(pl.BlockSpec((None, slab, cd), lambda j, i: (cl, j * n_m + i, 0)))
        args.append(cw)
        out_specs.append(pl.BlockSpec((slab, cd), lambda j, i: (j * n_m + i, 0)))
        out_shape.append(jax.ShapeDtypeStruct((ce, cd), BF16))
    return pl.pallas_call(
        functools.partial(_proj_kernel, act=act, with_ssq=with_ssq, with_cast=cast is not None),
        grid=(n_tiles, n_m),
        in_specs=in_specs,
        out_specs=out_specs,
        out_shape=out_shape,
        scratch_shapes=[pltpu.VMEM((k, tn), BF16)],
        compiler_params=_params("arbitrary", "arbitrary"),
        name="proj_" + str(act),
    )(*args)


def _out_kernel(y_ref, w_ref, x_ref, g_ref, *refs, n_tiles, tn, last):
    if last:
        o_ref, xrow = refs
    else:
        xo_ref, o_ref, xrow = refs
    n = pl.program_id(1)
    xn = x_ref[...] + jnp.dot(y_ref[...], w_ref[...], preferred_element_type=F32)
    if not last:
        xo_ref[...] = xn
    xrow[n] = xn

    @pl.when(n == n_tiles - 1)
    def _():
        ss = None
        for j in range(n_tiles):
            v = xrow[j]
            s = jnp.sum(v * v, axis=-1, keepdims=True)
            ss = s if ss is None else ss + s
        r = lax.rsqrt(ss * (1.0 / (n_tiles * tn)) + EPS)
        for j in range(n_tiles):
            cols = slice(j * tn, (j + 1) * tn)
            o_ref[:, cols] = (xrow[j] * r * g_ref[:, cols]).astype(o_ref.dtype)


def out_proj(y, w, x, g, last, tm=1024, tn=512):
    m, k = y.shape
    d = w.shape[1]
    n_tiles = d // tn
    in_specs = [pl.BlockSpec((tm, k), lambda i, n: (i, 0)),
                pl.BlockSpec((k, tn), lambda i, n: (0, n)),
                pl.BlockSpec((tm, tn), lambda i, n: (i, n)),
                pl.BlockSpec((1, d), lambda i, n: (0, 0))]
    row_spec = pl.BlockSpec((tm, d), lambda i, n: (i, 0))
    if last:
        out_specs = row_spec
        out_shape = jax.ShapeDtypeStruct((m, d), F32)
    else:
        out_specs = [pl.BlockSpec((tm, tn), lambda i, n: (i, n)), row_spec]
        out_shape = [jax.ShapeDtypeStruct((m, d), F32), jax.ShapeDtypeStruct((m, d), BF16)]
    return pl.pallas_call(
        functools.partial(_out_kernel, n_tiles=n_tiles, tn=tn, last=last),
        grid=(m // tm, n_tiles),
        in_specs=in_specs,
        out_specs=out_specs,
        out_shape=out_shape,
        scratch_shapes=[pltpu.VMEM((n_tiles, tm, tn), F32)],
        compiler_params=_params("parallel", "arbitrary"),
        name="out_proj",
    )(y, w, x, g.reshape(1, d))


AMIX_GROUPS_PER_STEP = 2


def _amix_kernel(h_ref, w_ref, u_ref, gv_ref, ss_ref, gain_ref, ws_ref, b_ref, o_ref, wbf_ref, t_ref,
                 *, tm):
    i, s = pl.program_id(0), pl.program_id(1)
    gps, c = AMIX_GROUPS_PER_STEP, A_GROUP_DIM

    @pl.when(i == 0)
    def _():
        for gg in range(gps):
            wbf_ref[s * gps + gg] = w_ref[:, gg * c:(gg + 1) * c].astype(BF16)

    ssq = ss_ref[0]
    for t in range(1, ss_ref.shape[0]):
        ssq = ssq + ss_ref[t]
    r = lax.rsqrt(ssq * (1.0 / D_INNER) + EPS)
    for gg in range(gps):
        cols = slice(gg * c, (gg + 1) * c)
        vn = (gv_ref[:, cols].astype(F32) * r * gain_ref[:, cols]).astype(BF16)
        wg = ws_ref[gg]
        bias = b_ref[gg]
        for ch in range(tm // CHUNK):
            rows = slice(ch * CHUNK, (ch + 1) * CHUNK)
            sv = jnp.dot(wg, vn[rows], preferred_element_type=F32) + bias
            t_ref[rows, cols] = _act(u_ref[rows, cols].astype(F32), "gelu") * sv
        sg = _act(jnp.dot(h_ref[...], wbf_ref[s * gps + gg], preferred_element_type=F32), "silu")
        o_ref[:, cols] = (t_ref[:, cols] * sg).astype(o_ref.dtype)


def amix(h, w_in, layer, gate_col0, u, gv, ssq, gain, w_s, b_s, tm=1024):
    m, e = u.shape
    k = h.shape[1]
    gps = AMIX_GROUPS_PER_STEP
    c = gps * A_GROUP_DIM
    n_steps = A_GROUPS // gps
    gb0 = gate_col0 // c
    n_ss = ssq.shape[0]
    grp = pl.BlockSpec((tm, c), lambda i, s: (i, s))
    return pl.pallas_call(
        functools.partial(_amix_kernel, tm=tm),
        grid=(m // tm, n_steps),
        in_specs=[pl.BlockSpec((tm, k), lambda i, s: (i, 0)),
                  pl.BlockSpec((None, k, c),
                               lambda i, s: (layer, 0, gb0 + jnp.where(i == 0, s, n_steps - 1)),
                               pipeline_mode=pl.Buffered(1)),
                  grp, grp,
                  pl.BlockSpec((n_ss, tm, 1), lambda i, s: (0, i, 0)),
                  pl.BlockSpec((1, c), lambda i, s: (0, s)),
                  pl.BlockSpec((gps, CHUNK, CHUNK), lambda i, s: (s, 0, 0)),
                  pl.BlockSpec((gps, CHUNK, 1), lambda i, s: (s, 0, 0))],
        out_specs=grp,
        out_shape=jax.ShapeDtypeStruct((m, e), BF16),
        scratch_shapes=[pltpu.VMEM((A_GROUPS, k, A_GROUP_DIM), BF16), pltpu.VMEM((tm, c), F32)],
        compiler_params=_params("arbitrary", "arbitrary"),
        name="amix",
    )(h, w_in, u, gv, ssq, gain.reshape(1, e), w_s.astype(BF16), b_s.reshape(A_GROUPS, CHUNK, 1))


def _dft_tables(n):
    idx = jnp.arange(n, dtype=jnp.int32)
    ang = ((idx[:, None] * idx[None, :]) % n).astype(F32) * (2.0 * jnp.pi / n)
    return jnp.cos(ang), jnp.sin(ang)


def _fold_kernel(cs_ref, w_ref, o_ref, *, scale):
    acc = jnp.dot(cs_ref[...], w_ref[...].astype(BF16), preferred_element_type=F32)
    o_ref[...] = (acc * scale).astype(o_ref.dtype)


def fold_channel_dft(w_mix, layer):
    _, g, c, _ = w_mix.shape
    cc, sc = _dft_tables(c)
    cs = jnp.concatenate([cc, sc], axis=0).astype(BF16)
    scale = float((SEQ * c) ** -0.5)
    return pl.pallas_call(
        functools.partial(_fold_kernel, scale=scale),
        grid=(g,),
        in_specs=[pl.BlockSpec((2 * c, c), lambda i: (0, 0)),
                  pl.BlockSpec((None, None, c, c), lambda i: (layer, i, 0, 0))],
        out_specs=pl.BlockSpec((None, 2 * c, c), lambda i: (i, 0, 0)),
        out_shape=jax.ShapeDtypeStruct((g, 2 * c, c), BF16),
        compiler_params=_params("parallel"),
        name="fold_channel_dft",
    )(cs, w_mix)


def _slab_dft_tables():
    k1 = jnp.arange(FFT_SLABS // 2 + 1, dtype=jnp.int32)[:, None, None]
    j = jnp.arange(FFT_SLAB, dtype=jnp.int32)[None, :, None]
    n2 = jnp.arange(FFT_SLAB, dtype=jnp.int32)[None, None, :]
    a1 = ((k1 * n2) % SEQ).astype(F32) * (2.0 * jnp.pi / SEQ)
    a2 = ((j * n2) % FFT_SLAB).astype(F32) * (2.0 * jnp.pi / FFT_SLAB)
    c1, s1, c2, s2 = jnp.cos(a1), jnp.sin(a1), jnp.cos(a2), jnp.sin(a2)
    cos, sin = c1 * c2 - s1 * s2, s1 * c2 + c1 * s2
    top = jnp.concatenate([cos, sin], axis=2)
    bot = jnp.concatenate([-sin, cos], axis=2)
    return jnp.concatenate([top, bot], axis=1).astype(BF16)


def _interleave_table():
    k1, r = np.divmod(np.arange(FFT_SLAB), FFT_SLABS)
    k2 = np.where(k1 <= FFT_SLABS // 2, r, FFT_SLABS - 1 - r)
    p = np.zeros((FFT_SLAB, FFT_SLAB), np.float32)
    p[FFT_SLABS * k2 + k1, np.arange(FFT_SLAB)] = 1.0
    return jnp.asarray(p, dtype=BF16)


_STAGE1_BLOCKS = ([(0, 0)] + [(k1, part) for k1 in range(1, FFT_SLABS // 2) for part in (0, 1)]
                  + [(FFT_SLABS // 2, 0)])


def _stage1_table():
    k1 = jnp.asarray([b[0] for b in _STAGE1_BLOCKS], jnp.int32)[:, None]
    is_im = jnp.asarray([b[1] for b in _STAGE1_BLOCKS], jnp.int32)[:, None]
    m = jnp.arange(FFT_SLABS, dtype=jnp.int32)[None, :]
    ang = ((k1 * m) % FFT_SLABS).astype(F32) * (2.0 * jnp.pi / FFT_SLABS)
    coef = jnp.where(is_im == 1, -jnp.sin(ang), jnp.cos(ang))
    return jnp.kron(coef, jnp.eye(BF16_ROWS, dtype=F32)).astype(BF16)


def _fftmix_kernel(x_ref, m1_ref, g_ref, wf_ref, p_ref, sg_ref, o_ref, a_ref, ym_ref, *, tc):
    half = FFT_SLABS // 2
    c = B_GROUP_DIM

    for grp in range(FFT_SLAB // BF16_ROWS):
        r0 = grp * BF16_ROWS
        xg = jnp.concatenate([x_ref[m * FFT_SLAB + r0:m * FFT_SLAB + r0 + BF16_ROWS, :]
                              for m in range(FFT_SLABS)], axis=0)
        s1 = jnp.dot(m1_ref[...], xg, preferred_element_type=F32).astype(BF16)
        for blk, (k1, part) in enumerate(_STAGE1_BLOCKS):
            a_ref[k1, part * FFT_SLAB + r0:part * FFT_SLAB + r0 + BF16_ROWS, :] = (
                s1[blk * BF16_ROWS:(blk + 1) * BF16_ROWS])

    for k1 in range(half + 1):
        if k1 % half == 0:
            y = jnp.dot(g_ref[k1, :, :FFT_SLAB], a_ref[k1, :FFT_SLAB, :], preferred_element_type=F32)
        else:
            y = jnp.dot(g_ref[k1], a_ref[k1], preferred_element_type=F32)
        y = y.astype(BF16)
        for g in range(tc // c):
            cols = slice(g * c, (g + 1) * c)
            p_re = jnp.dot(y[:FFT_SLAB, cols], wf_ref[g, :c, :], preferred_element_type=F32)
            p_im = jnp.dot(y[FFT_SLAB:, cols], wf_ref[g, c:, :], preferred_element_type=F32)
            fwd = (p_re + p_im).astype(BF16)
            for k3 in range(FFT_SLABS):
                ym_ref[k3, k1 * FFT_SLABS:(k1 + 1) * FFT_SLABS, cols] = (
                    fwd[k3 * FFT_SLABS:(k3 + 1) * FFT_SLABS])
            if k1 % half != 0:
                mir = (p_re - p_im).astype(BF16)
                km = FFT_SLABS - k1
                for k3 in range(FFT_SLABS):
                    src = FFT_SLABS - 1 - k3
                    ym_ref[k3, km * FFT_SLABS:(km + 1) * FFT_SLABS, cols] = (
                        mir[src * FFT_SLABS:(src + 1) * FFT_SLABS])

    for k3 in range(FFT_SLABS):
        rows = slice(k3 * FFT_SLAB, (k3 + 1) * FFT_SLAB)
        z = jnp.dot(p_ref[...], ym_ref[k3], preferred_element_type=F32)
        o_ref[rows, :] = (z * sg_ref[rows, :].astype(F32)).astype(o_ref.dtype)


def fftmix(xb, wf, sg, bsz, tc=512):
    e = xb.shape[1]
    gpt = tc // B_GROUP_DIM
    tile = pl.BlockSpec((None, SEQ, tc), lambda b, j: (b, 0, j))
    once = pl.Buffered(1)
    out = pl.pallas_call(
        functools.partial(_fftmix_kernel, tc=tc),
        grid=(bsz, e // tc),
        in_specs=[tile,
                  pl.BlockSpec((FFT_SLAB, FFT_SLAB), lambda b, j: (0, 0), pipeline_mode=once),
                  pl.BlockSpec((FFT_SLABS // 2 + 1, 2 * FFT_SLAB, 2 * FFT_SLAB), lambda b, j: (0, 0, 0),
                               pipeline_mode=once),
                  pl.BlockSpec((gpt, 2 * B_GROUP_DIM, B_GROUP_DIM), lambda b, j: (j, 0, 0)),
                  pl.BlockSpec((FFT_SLAB, FFT_SLAB), lambda b, j: (0, 0), pipeline_mode=once),
                  tile],
        out_specs=tile,
        out_shape=jax.ShapeDtypeStruct((bsz, SEQ, e), BF16),
        scratch_shapes=[pltpu.VMEM((FFT_SLABS // 2 + 1, 2 * FFT_SLAB, tc), BF16),
                        pltpu.VMEM((FFT_SLABS, FFT_SLAB, tc), BF16)],
        compiler_params=_params("parallel", "parallel"),
        name="fftmix",
    )(xb.reshape(bsz, SEQ, e), _stage1_table(), _slab_dft_tables(), wf, _interleave_table(),
      sg.reshape(bsz, SEQ, e))
    return out.reshape(bsz * SEQ, e)


POOL_ROWS = 128
POOL_K = 256
FOLD_ROWS = 1024


def _cfold_kernel(wc_ref, wm_ref, o_ref, wmbf_ref):
    @pl.when(pl.program_id(1) == 0)
    def _():
        wmbf_ref[...] = wm_ref[...].astype(BF16)

    o_ref[...] = jnp.dot(wc_ref[...].astype(BF16), wmbf_ref[...],
                         preferred_element_type=F32).astype(o_ref.dtype)


def fold_pool_weights(w_in, w_mix, layer):
    _, d, _ = w_in.shape
    c = C_GROUP_DIM
    return pl.pallas_call(
        _cfold_kernel,
        grid=(C_GROUPS, d // FOLD_ROWS),
        in_specs=[pl.BlockSpec((None, FOLD_ROWS, c), lambda g, t: (layer, t, g)),
                  pl.BlockSpec((None, None, c, c), lambda g, t: (layer, g, 0, 0))],
        out_specs=pl.BlockSpec((None, FOLD_ROWS, c), lambda g, t: (g, t, 0)),
        out_shape=jax.ShapeDtypeStruct((C_GROUPS, d, c), BF16),
        scratch_shapes=[pltpu.VMEM((c, c), BF16)],
        compiler_params=_params("arbitrary", "arbitrary"),
        name="fold_pool_weights",
    )(w_in, w_mix)


def _cmix_kernel(prev_ref, cur_ref, next_ref, wf_ref, wg_ref, scale_ref, cast_in, o_ref, cast_out,
                 wgbf_ref, *, tm):
    g = pl.program_id(0)
    i = pl.program_id(1)
    cast_out[...] = cast_in[...].astype(cast_out.dtype)

    @pl.when(i == 0)
    def _():
        wgbf_ref[...] = wg_ref[...].astype(BF16)

    win = jnp.left_shift(2, g)
    lo_off = win // 2
    hi_off = win - 1 - lo_off
    t0 = (i % (SEQ // tm)) * tm
    ext = jnp.concatenate([prev_ref[...], cur_ref[...], next_ref[...]], axis=0)
    sg = _act(jnp.dot(cur_ref[...], wgbf_ref[...], preferred_element_type=F32), "silu")
    pooled = []
    for j in range(tm // POOL_ROWS):
        start = min(j * POOL_ROWS, tm + 2 * HALO - POOL_K)
        t = t0 + j * POOL_ROWS + lax.broadcasted_iota(jnp.int32, (POOL_ROWS, 1), 0)
        u = t0 + (start - HALO) + lax.broadcasted_iota(jnp.int32, (1, POOL_K), 1)
        lo = jnp.maximum(t - lo_off, 0)
        hi = jnp.minimum(t + hi_off, SEQ - 1)
        band = jnp.where((u >= lo) & (u <= hi), 1.0, 0.0).astype(BF16)
        wsum = jnp.dot(band, ext[start:start + POOL_K], preferred_element_type=F32)
        cnt = (hi - lo + 1).astype(F32)
        rows = slice(j * POOL_ROWS, (j + 1) * POOL_ROWS)
        pooled.append((wsum / cnt - cur_ref[rows, :].astype(F32)).astype(BF16))
    hp = jnp.concatenate(pooled, axis=0)
    acc = jnp.dot(hp, wf_ref[...], preferred_element_type=F32)
    o_ref[...] = (acc * scale_ref[...] * sg).astype(o_ref.dtype)


def cmix(h, wfold, w_in, layer, gate_col0, scale, w_out, tm=1024):
    m, k = h.shape
    c = C_GROUP_DIM
    e = C_GROUPS * c
    gb0 = gate_col0 // c
    hb = tm // HALO
    last_hb = m // HALO - 1
    n_m = m // tm
    _, ce, cd = w_out.shape
    slab = ce // (C_GROUPS * n_m)
    assert slab * C_GROUPS * n_m == ce and slab % BF16_ROWS == 0
    return pl.pallas_call(
        functools.partial(_cmix_kernel, tm=tm),
        grid=(C_GROUPS, m // tm),
        in_specs=[pl.BlockSpec((HALO, k), lambda g, i: (jnp.maximum(i * hb - 1, 0), 0)),
                  pl.BlockSpec((tm, k), lambda g, i: (i, 0)),
                  pl.BlockSpec((HALO, k), lambda g, i: (jnp.minimum((i + 1) * hb, last_hb), 0)),
                  pl.BlockSpec((None, k, c), lambda g, i: (g, 0, 0)),
                  pl.BlockSpec((None, k, c), lambda g, i: (layer, 0, gb0 + g),
                               pipeline_mode=pl.Buffered(1)),
                  pl.BlockSpec((1, c), lambda g, i: (0, g)),
                  pl.BlockSpec((None, slab, cd), lambda g, i: (layer, g * n_m + i, 0))],
        out_specs=[pl.BlockSpec((tm, c), lambda g, i: (i, g)),
                   pl.BlockSpec((slab, cd), lambda g, i: (g * n_m + i, 0))],
        out_shape=[jax.ShapeDtypeStruct((m, e), BF16), jax.ShapeDtypeStruct((ce, cd), BF16)],
        scratch_shapes=[pltpu.VMEM((k, c), BF16)],
        compiler_params=_params("arbitrary", "arbitrary"),
        name="cmix",
    )(h, h, h, wfold, w_in, scale.reshape(1, e), w_out)


def kernel(x, a_norm, a_w_in, a_v_gain, a_w_s, a_b_s, a_w_out, b_norm, b_w_in, b_w_mix, b_w_out,
           c_norm, c_w_in, c_w_mix, c_scale, c_w_out, final_norm):
    bsz, seq, d = x.shape
    assert (seq, d) == (SEQ, D_MODEL)
    e = D_INNER
    depth = 4
    norms = {0: a_norm, 1: b_norm, 2: c_norm}

    def gain_for(i):
        if i == depth:
            return final_norm
        return norms[i % 3][i // 3]

    xf = x.reshape(bsz * seq, d)
    h = rms_cast(xf, gain_for(0))
    for i in range(depth):
        kind, j = i % 3, i // 3
        if kind == 0:
            (u,) = proj(h, a_w_in, j, 0, e, None, BF16, tm=2048)
            gv, ssq, w_out = proj(h, a_w_in, j, e, e, "gelu", BF16, with_ssq=True, cast=(a_w_out, j))
            y = amix(h, a_w_in, j, 2 * e, u, gv, ssq, a_v_gain[j], a_w_s[j], a_b_s[j])
        elif kind == 1:
            (xb,) = proj(h, b_w_in, j, 0, e, None, BF16, tm=2048)
            sg, w_out = proj(h, b_w_in, j, e, e, "silu", BF16, cast=(b_w_out, j))
            wf = fold_channel_dft(b_w_mix, j)
            y = fftmix(xb, wf, sg, bsz)
        else:
            wfold = fold_pool_weights(c_w_in, c_w_mix, j)
            y, w_out = cmix(h, wfold, c_w_in, j, e, c_scale[j], c_w_out)
        last = i == depth - 1
        res = out_proj(y, w_out, xf, gain_for(i + 1), last)
        if last:
            return res.reshape(bsz, seq, d)
        xf, h = res
```

```python
import functools

import numpy as np
import jax
import jax.numpy as jnp
from jax import lax
from jax.experimental import pallas as pl
from jax.experimental.pallas import tpu as pltpu

D_MODEL = 2048
D_INNER = 4096
SEQ = 4096
CHUNK = 128
A_GROUPS = 8
A_GROUP_DIM = D_INNER // A_GROUPS
B_GROUPS = 8
B_GROUP_DIM = D_INNER // B_GROUPS
POOL_WINDOWS = (2, 4, 8, 16)
C_GROUPS = len(POOL_WINDOWS)
assert POOL_WINDOWS == tuple(2 << g for g in range(C_GROUPS))
C_GROUP_DIM = D_INNER // C_GROUPS
EPS = 1e-6
HALO = 16

FFT_SLABS = 16
FFT_SLAB = SEQ // FFT_SLABS
BF16_ROWS = 16

V7X_VMEM_BYTES = 64 * 1024 * 1024
VMEM_LIMIT = V7X_VMEM_BYTES - 8 * 1024 * 1024

F32 = jnp.float32
BF16 = jnp.bfloat16


def _params(*sem):
    return pltpu.CompilerParams(dimension_semantics=sem, vmem_limit_bytes=VMEM_LIMIT)


def _norm_kernel(x_ref, g_ref, h_ref):
    x = x_ref[...]
    r = lax.rsqrt(jnp.mean(x * x, axis=-1, keepdims=True) + EPS)
    h_ref[...] = (x * r * g_ref[...]).astype(h_ref.dtype)


def rms_cast(x, g, tm=1024):
    m, d = x.shape
    return pl.pallas_call(
        _norm_kernel,
        grid=(m // tm,),
        in_specs=[pl.BlockSpec((tm, d), lambda i: (i, 0)),
                  pl.BlockSpec((1, d), lambda i: (0, 0))],
        out_specs=pl.BlockSpec((tm, d), lambda i: (i, 0)),
        out_shape=jax.ShapeDtypeStruct((m, d), BF16),
        compiler_params=_params("parallel"),
        name="rms_cast",
    )(x, g.reshape(1, d))


GELU_C = 0.7978845608028654
GELU_A = 0.044715


def _act(x, act):
    if act is None:
        return x
    hx = 0.5 * x
    if act == "gelu":
        t = jnp.tanh(x * (GELU_C + (GELU_C * GELU_A) * (x * x)))
    else:
        t = jnp.tanh(hx)
    return hx + hx * t


def _proj_kernel(a_ref, w_ref, *refs, act, with_ssq, with_cast):
    refs = list(refs)
    cast_in = refs.pop(0) if with_cast else None
    o_ref = refs.pop(0)
    ss_ref = refs.pop(0) if with_ssq else None
    cast_out = refs.pop(0) if with_cast else None
    (wbf_ref,) = refs

    @pl.when(pl.program_id(1) == 0)
    def _():
        wbf_ref[...] = w_ref[...].astype(BF16)

    val = _act(jnp.dot(a_ref[...], wbf_ref[...], preferred_element_type=F32), act)
    o_ref[...] = val.astype(o_ref.dtype)
    if with_ssq:
        ss_ref[...] = jnp.sum(val * val, axis=-1, keepdims=True)
    if with_cast:
        cast_out[...] = cast_in[...].astype(cast_out.dtype)


def proj(a, w, layer, col0, ncols, act, out_dtype, with_ssq=False, cast=None, tm=1024, tn=1024):
    m, k = a.shape
    nb0 = col0 // tn
    n_tiles = ncols // tn
    n_m = m // tm
    in_specs = [pl.BlockSpec((tm, k), lambda j, i: (i, 0)),
                pl.BlockSpec((None, k, tn), lambda j, i: (layer, 0, nb0 + j))]
    args = [a, w]
    out_specs = [pl.BlockSpec((tm, tn), lambda j, i: (i, j))]
    out_shape = [jax.ShapeDtypeStruct((m, ncols), out_dtype)]
    if with_ssq:
        out_specs.append(pl.BlockSpec((None, tm, 1), lambda j, i: (j, i, 0)))
        out_shape.append(jax.ShapeDtypeStruct((n_tiles, m, 1), F32))
    if cast is not None:
        cw, cl = cast
        _, ce, cd = cw.shape
        slab = ce // (n_tiles * n_m)
        assert slab * n_tiles * n_m == ce and slab % BF16_ROWS == 0
        in_specs.append(pl.BlockSpec((None, slab, cd), lambda j, i: (cl, j * n_m + i, 0)))
        args.append(cw)
        out_specs.append(pl.BlockSpec((slab, cd), lambda j, i: (j * n_m + i, 0)))
        out_shape.append(jax.ShapeDtypeStruct((ce, cd), BF16))
    return pl.pallas_call(
        functools.partial(_proj_kernel, act=act, with_ssq=with_ssq, with_cast=cast is not None),
        grid=(n_tiles, n_m),
        in_specs=in_specs,
        out_specs=out_specs,
        out_shape=out_shape,
        scratch_shapes=[pltpu.VMEM((k, tn), BF16)],
        compiler_params=_params("arbitrary", "arbitrary"),
        name="proj_" + str(act),
    )(*args)


def _staged_weight_cast(w_hbm, stage_ref, sem, layer, col0, ncols, blk, n_blk, first, cast):
    def copy(b):
        start = pl.multiple_of(col0 + b * ncols, ncols)
        return pltpu.make_async_copy(w_hbm.at[layer, :, pl.ds(start, ncols)], stage_ref, sem)

    @pl.when(first & (blk == 0))
    def _():
        copy(blk).start()

    @pl.when(first)
    def _():
        copy(blk).wait()
        cast()

    @pl.when(first & (blk + 1 < n_blk))
    def _():
        copy(blk + 1).start()


def _out_kernel(y_ref, w_ref, x_ref, g_ref, *refs, n_tiles, tn, last):
    if last:
        o_ref, xrow = refs
    else:
        xo_ref, o_ref, xrow = refs
    n = pl.program_id(1)
    xn = x_ref[...] + jnp.dot(y_ref[...], w_ref[...], preferred_element_type=F32)
    if not last:
        xo_ref[...] = xn
    xrow[n] = xn

    @pl.when(n == n_tiles - 1)
    def _():
        ss = None
        for j in range(n_tiles):
            v = xrow[j]
            s = jnp.sum(v * v, axis=-1, keepdims=True)
            ss = s if ss is None else ss + s
        r = lax.rsqrt(ss * (1.0 / (n_tiles * tn)) + EPS)
        for j in range(n_tiles):
            cols = slice(j * tn, (j + 1) * tn)
            o_ref[:, cols] = (xrow[j] * r * g_ref[:, cols]).astype(o_ref.dtype)


def out_proj(y, w, x, g, last, tm=1024, tn=512):
    m, k = y.shape
    d = w.shape[1]
    n_tiles = d // tn
    in_specs = [pl.BlockSpec((tm, k), lambda i, n: (i, 0)),
                pl.BlockSpec((k, tn), lambda i, n: (0, n)),
                pl.BlockSpec((tm, tn), lambda i, n: (i, n)),
                pl.BlockSpec((1, d), lambda i, n: (0, 0))]
    row_spec = pl.BlockSpec((tm, d), lambda i, n: (i, 0))
    if last:
        out_specs = row_spec
        out_shape = jax.ShapeDtypeStruct((m, d), F32)
    else:
        out_specs = [pl.BlockSpec((tm, tn), lambda i, n: (i, n)), row_spec]
        out_shape = [jax.ShapeDtypeStruct((m, d), F32), jax.ShapeDtypeStruct((m, d), BF16)]
    return pl.pallas_call(
        functools.partial(_out_kernel, n_tiles=n_tiles, tn=tn, last=last),
        grid=(m // tm, n_tiles),
        in_specs=in_specs,
        out_specs=out_specs,
        out_shape=out_shape,
        scratch_shapes=[pltpu.VMEM((n_tiles, tm, tn), F32)],
        compiler_params=_params("parallel", "arbitrary"),
        name="out_proj",
    )(y, w, x, g.reshape(1, d))


AMIX_GROUPS_PER_STEP = 2


def _amix_kernel(h_ref, w_hbm, u_ref, gv_ref, ss_ref, gain_ref, ws_ref, b_ref, o_ref, wbf_ref, t_ref,
                 stage_ref, sem, *, tm, layer, gate_col0):
    i, s = pl.program_id(0), pl.program_id(1)
    gps, c = AMIX_GROUPS_PER_STEP, A_GROUP_DIM

    def cast():
        for gg in range(gps):
            wbf_ref[s * gps + gg] = stage_ref[:, gg * c:(gg + 1) * c].astype(BF16)

    _staged_weight_cast(w_hbm, stage_ref, sem, layer, gate_col0, gps * c, s, A_GROUPS // gps, i == 0, cast)

    ssq = ss_ref[0]
    for t in range(1, ss_ref.shape[0]):
        ssq = ssq + ss_ref[t]
    r = lax.rsqrt(ssq * (1.0 / D_INNER) + EPS)
    for gg in range(gps):
        cols = slice(gg * c, (gg + 1) * c)
        vn = (gv_ref[:, cols].astype(F32) * r * gain_ref[:, cols]).astype(BF16)
        wg = ws_ref[gg]
        bias = b_ref[gg]
        for ch in range(tm // CHUNK):
            rows = slice(ch * CHUNK, (ch + 1) * CHUNK)
            sv = jnp.dot(wg, vn[rows], preferred_element_type=F32) + bias
            t_ref[rows, cols] = _act(u_ref[rows, cols].astype(F32), "gelu") * sv
        sg = _act(jnp.dot(h_ref[...], wbf_ref[s * gps + gg], preferred_element_type=F32), "silu")
        o_ref[:, cols] = (t_ref[:, cols] * sg).astype(o_ref.dtype)


def amix(h, w_in, layer, gate_col0, u, gv, ssq, gain, w_s, b_s, tm=1024):
    m, e = u.shape
    k = h.shape[1]
    gps = AMIX_GROUPS_PER_STEP
    c = gps * A_GROUP_DIM
    n_steps = A_GROUPS // gps
    n_ss = ssq.shape[0]
    grp = pl.BlockSpec((tm, c), lambda i, s: (i, s))
    return pl.pallas_call(
        functools.partial(_amix_kernel, tm=tm, layer=layer, gate_col0=gate_col0),
        grid=(m // tm, n_steps),
        in_specs=[pl.BlockSpec((tm, k), lambda i, s: (i, 0)),
                  pl.BlockSpec(memory_space=pl.ANY),
                  grp, grp,
                  pl.BlockSpec((n_ss, tm, 1), lambda i, s: (0, i, 0)),
                  pl.BlockSpec((1, c), lambda i, s: (0, s)),
                  pl.BlockSpec((gps, CHUNK, CHUNK), lambda i, s: (s, 0, 0)),
                  pl.BlockSpec((gps, CHUNK, 1), lambda i, s: (s, 0, 0))],
        out_specs=grp,
        out_shape=jax.ShapeDtypeStruct((m, e), BF16),
        scratch_shapes=[pltpu.VMEM((A_GROUPS, k, A_GROUP_DIM), BF16), pltpu.VMEM((tm, c), F32),
                        pltpu.VMEM((k, c), F32), pltpu.SemaphoreType.DMA(())],
        compiler_params=_params("arbitrary", "arbitrary"),
        name="amix",
    )(h, w_in, u, gv, ssq, gain.reshape(1, e), w_s.astype(BF16), b_s.reshape(A_GROUPS, CHUNK, 1))


def _dft_tables(n):
    idx = jnp.arange(n, dtype=jnp.int32)
    ang = ((idx[:, None] * idx[None, :]) % n).astype(F32) * (2.0 * jnp.pi / n)
    return jnp.cos(ang), jnp.sin(ang)


def _fold_kernel(cs_ref, w_ref, o_ref, *, scale):
    acc = jnp.dot(cs_ref[...], w_ref[...].astype(BF16), preferred_element_type=F32)
    o_ref[...] = (acc * scale).astype(o_ref.dtype)


def fold_channel_dft(w_mix, layer):
    _, g, c, _ = w_mix.shape
    cc, sc = _dft_tables(c)
    cs = jnp.concatenate([cc, sc], axis=0).astype(BF16)
    scale = float((SEQ * c) ** -0.5)
    return pl.pallas_call(
        functools.partial(_fold_kernel, scale=scale),
        grid=(g,),
        in_specs=[pl.BlockSpec((2 * c, c), lambda i: (0, 0)),
                  pl.BlockSpec((None, None, c, c), lambda i: (layer, i, 0, 0))],
        out_specs=pl.BlockSpec((None, 2 * c, c), lambda i: (i, 0, 0)),
        out_shape=jax.ShapeDtypeStruct((g, 2 * c, c), BF16),
        compiler_params=_params("parallel"),
        name="fold_channel_dft",
    )(cs, w_mix)


def _slab_dft_tables():
    k1 = jnp.arange(FFT_SLABS // 2 + 1, dtype=jnp.int32)[:, None, None]
    j = jnp.arange(FFT_SLAB, dtype=jnp.int32)[None, :, None]
    n2 = jnp.arange(FFT_SLAB, dtype=jnp.int32)[None, None, :]
    a1 = ((k1 * n2) % SEQ).astype(F32) * (2.0 * jnp.pi / SEQ)
    a2 = ((j * n2) % FFT_SLAB).astype(F32) * (2.0 * jnp.pi / FFT_SLAB)
    c1, s1, c2, s2 = jnp.cos(a1), jnp.sin(a1), jnp.cos(a2), jnp.sin(a2)
    cos, sin = c1 * c2 - s1 * s2, s1 * c2 + c1 * s2
    top = jnp.concatenate([cos, sin], axis=2)
    bot = jnp.concatenate([-sin, cos], axis=2)
    return jnp.concatenate([top, bot], axis=1).astype(BF16)


def _interleave_table():
    k1, r = np.divmod(np.arange(FFT_SLAB), FFT_SLABS)
    k2 = np.where(k1 <= FFT_SLABS // 2, r, FFT_SLABS - 1 - r)
    p = np.zeros((FFT_SLAB, FFT_SLAB), np.float32)
    p[FFT_SLABS * k2 + k1, np.arange(FFT_SLAB)] = 1.0
    return jnp.asarray(p, dtype=BF16)


_STAGE1_BLOCKS = ([(0, 0)] + [(k1, part) for k1 in range(1, FFT_SLABS // 2) for part in (0, 1)]
                  + [(FFT_SLABS // 2, 0)])


def _stage1_table():
    k1 = jnp.asarray([b[0] for b in _STAGE1_BLOCKS], jnp.int32)[:, None]
    is_im = jnp.asarray([b[1] for b in _STAGE1_BLOCKS], jnp.int32)[:, None]
    m = jnp.arange(FFT_SLABS, dtype=jnp.int32)[None, :]
    ang = ((k1 * m) % FFT_SLABS).astype(F32) * (2.0 * jnp.pi / FFT_SLABS)
    coef = jnp.where(is_im == 1, -jnp.sin(ang), jnp.cos(ang))
    return jnp.kron(coef, jnp.eye(BF16_ROWS, dtype=F32)).astype(BF16)


def _fftmix_kernel(x_ref, m1_ref, g_ref, wf_ref, p_ref, sg_ref, o_ref, a_ref, ym_ref, *, tc):
    half = FFT_SLABS // 2
    c = B_GROUP_DIM

    for grp in range(FFT_SLAB // BF16_ROWS):
        r0 = grp * BF16_ROWS
        xg = jnp.concatenate([x_ref[m * FFT_SLAB + r0:m * FFT_SLAB + r0 + BF16_ROWS, :]
                              for m in range(FFT_SLABS)], axis=0)
        s1 = jnp.dot(m1_ref[...], xg, preferred_element_type=F32).astype(BF16)
        for blk, (k1, part) in enumerate(_STAGE1_BLOCKS):
            a_ref[k1, part * FFT_SLAB + r0:part * FFT_SLAB + r0 + BF16_ROWS, :] = (
                s1[blk * BF16_ROWS:(blk + 1) * BF16_ROWS])

    for k1 in range(half + 1):
        if k1 % half == 0:
            y = jnp.dot(g_ref[k1, :, :FFT_SLAB], a_ref[k1, :FFT_SLAB, :], preferred_element_type=F32)
        else:
            y = jnp.dot(g_ref[k1], a_ref[k1], preferred_element_type=F32)
        y = y.astype(BF16)
        for g in range(tc // c):
            cols = slice(g * c, (g + 1) * c)
            p_re = jnp.dot(y[:FFT_SLAB, cols], wf_ref[g, :c, :], preferred_element_type=F32)
            p_im = jnp.dot(y[FFT_SLAB:, cols], wf_ref[g, c:, :], preferred_element_type=F32)
            fwd = (p_re + p_im).astype(BF16)
            for k3 in range(FFT_SLABS):
                ym_ref[k3, k1 * FFT_SLABS:(k1 + 1) * FFT_SLABS, cols] = (
                    fwd[k3 * FFT_SLABS:(k3 + 1) * FFT_SLABS])
            if k1 % half != 0:
                mir = (p_re - p_im).astype(BF16)
                km = FFT_SLABS - k1
                for k3 in range(FFT_SLABS):
                    src = FFT_SLABS - 1 - k3
                    ym_ref[k3, km * FFT_SLABS:(km + 1) * FFT_SLABS, cols] = (
                        mir[src * FFT_SLABS:(src + 1) * FFT_SLABS])

    for k3 in range(FFT_SLABS):
        rows = slice(k3 * FFT_SLAB, (k3 + 1) * FFT_SLAB)
        z = jnp.dot(p_ref[...], ym_ref[k3], preferred_element_type=F32)
        o_ref[rows, :] = (z * sg_ref[rows, :].astype(F32)).astype(o_ref.dtype)


def fftmix(xb, wf, sg, bsz, tc=512):
    e = xb.shape[1]
    gpt = tc // B_GROUP_DIM
    tile = pl.BlockSpec((None, SEQ, tc), lambda b, j: (b, 0, j))
    once = pl.Buffered(1)
    out = pl.pallas_call(
        functools.partial(_fftmix_kernel, tc=tc),
        grid=(bsz, e // tc),
        in_specs=[tile,
                  pl.BlockSpec((FFT_SLAB, FFT_SLAB), lambda b, j: (0, 0), pipeline_mode=once),
                  pl.BlockSpec((FFT_SLABS // 2 + 1, 2 * FFT_SLAB, 2 * FFT_SLAB), lambda b, j: (0, 0, 0),
                               pipeline_mode=once),
                  pl.BlockSpec((gpt, 2 * B_GROUP_DIM, B_GROUP_DIM), lambda b, j: (j, 0, 0)),
                  pl.BlockSpec((FFT_SLAB, FFT_SLAB), lambda b, j: (0, 0), pipeline_mode=once),
                  tile],
        out_specs=tile,
        out_shape=jax.ShapeDtypeStruct((bsz, SEQ, e), BF16),
        scratch_shapes=[pltpu.VMEM((FFT_SLABS // 2 + 1, 2 * FFT_SLAB, tc), BF16),
                        pltpu.VMEM((FFT_SLABS, FFT_SLAB, tc), BF16)],
        compiler_params=_params("parallel", "parallel"),
        name="fftmix",
    )(xb.reshape(bsz, SEQ, e), _stage1_table(), _slab_dft_tables(), wf, _interleave_table(),
      sg.reshape(bsz, SEQ, e))
    return out.reshape(bsz * SEQ, e)


POOL_ROWS = 128
POOL_K = 256
FOLD_ROWS = 1024


def _cfold_kernel(wc_ref, wm_ref, o_ref, wmbf_ref):
    @pl.when(pl.program_id(1) == 0)
    def _():
        wmbf_ref[...] = wm_ref[...].astype(BF16)

    o_ref[...] = jnp.dot(wc_ref[...].astype(BF16), wmbf_ref[...],
                         preferred_element_type=F32).astype(o_ref.dtype)


def fold_pool_weights(w_in, w_mix, layer):
    _, d, _ = w_in.shape
    c = C_GROUP_DIM
    return pl.pallas_call(
        _cfold_kernel,
        grid=(C_GROUPS, d // FOLD_ROWS),
        in_specs=[pl.BlockSpec((None, FOLD_ROWS, c), lambda g, t: (layer, t, g)),
                  pl.BlockSpec((None, None, c, c), lambda g, t: (layer, g, 0, 0))],
        out_specs=pl.BlockSpec((None, FOLD_ROWS, c), lambda g, t: (g, t, 0)),
        out_shape=jax.ShapeDtypeStruct((C_GROUPS, d, c), BF16),
        scratch_shapes=[pltpu.VMEM((c, c), BF16)],
        compiler_params=_params("arbitrary", "arbitrary"),
        name="fold_pool_weights",
    )(w_in, w_mix)


def _cmix_kernel(prev_ref, cur_ref, next_ref, wf_ref, w_hbm, scale_ref, cast_in, o_ref, cast_out,
                 wgbf_ref, stage_ref, sem, *, tm, layer, gate_col0):
    g = pl.program_id(0)
    i = pl.program_id(1)
    cast_out[...] = cast_in[...].astype(cast_out.dtype)

    def cast():
        wgbf_ref[...] = stage_ref[...].astype(BF16)

    _staged_weight_cast(w_hbm, stage_ref, sem, layer, gate_col0, C_GROUP_DIM, g, C_GROUPS, i == 0, cast)

    win = jnp.left_shift(2, g)
    lo_off = win // 2
    hi_off = win - 1 - lo_off
    t0 = (i % (SEQ // tm)) * tm
    ext = jnp.concatenate([prev_ref[...], cur_ref[...], next_ref[...]], axis=0)
    sg = _act(jnp.dot(cur_ref[...], wgbf_ref[...], preferred_element_type=F32), "silu")
    pooled = []
    for j in range(tm // POOL_ROWS):
        start = min(j * POOL_ROWS, tm + 2 * HALO - POOL_K)
        t = t0 + j * POOL_ROWS + lax.broadcasted_iota(jnp.int32, (POOL_ROWS, 1), 0)
        u = t0 + (start - HALO) + lax.broadcasted_iota(jnp.int32, (1, POOL_K), 1)
        lo = jnp.maximum(t - lo_off, 0)
        hi = jnp.minimum(t + hi_off, SEQ - 1)
        band = jnp.where((u >= lo) & (u <= hi), 1.0, 0.0).astype(BF16)
        wsum = jnp.dot(band, ext[start:start + POOL_K], preferred_element_type=F32)
        cnt = (hi - lo + 1).astype(F32)
        rows = slice(j * POOL_ROWS, (j + 1) * POOL_ROWS)
        pooled.append((wsum / cnt - cur_ref[rows, :].astype(F32)).astype(BF16))
    hp = jnp.concatenate(pooled, axis=0)
    acc = jnp.dot(hp, wf_ref[...], preferred_element_type=F32)
    o_ref[...] = (acc * scale_ref[...] * sg).astype(o_ref.dtype)


def cmix(h, wfold, w_in, layer, gate_col0, scale, w_out, tm=1024):
    m, k = h.shape
    c = C_GROUP_DIM
    e = C_GROUPS * c
    hb = tm // HALO
    last_hb = m // HALO - 1
    n_m = m // tm
    _, ce, cd = w_out.shape
    slab = ce // (C_GROUPS * n_m)
    assert slab * C_GROUPS * n_m == ce and slab % BF16_ROWS == 0
    return pl.pallas_call(
        functools.partial(_cmix_kernel, tm=tm, layer=layer, gate_col0=gate_col0),
        grid=(C_GROUPS, m // tm),
        in_specs=[pl.BlockSpec((HALO, k), lambda g, i: (jnp.maximum(i * hb - 1, 0), 0)),
                  pl.BlockSpec((tm, k), lambda g, i: (i, 0)),
                  pl.BlockSpec((HALO, k), lambda g, i: (jnp.minimum((i + 1) * hb, last_hb), 0)),
                  pl.BlockSpec((None, k, c), lambda g, i: (g, 0, 0)),
                  pl.BlockSpec(memory_space=pl.ANY),
                  pl.BlockSpec((1, c), lambda g, i: (0, g)),
                  pl.BlockSpec((None, slab, cd), lambda g, i: (layer, g * n_m + i, 0))],
        out_specs=[pl.BlockSpec((tm, c), lambda g, i: (i, g)),
                   pl.BlockSpec((slab, cd), lambda g, i: (g * n_m + i, 0))],
        out_shape=[jax.ShapeDtypeStruct((m, e), BF16), jax.ShapeDtypeStruct((ce, cd), BF16)],
        scratch_shapes=[pltpu.VMEM((k, c), BF16), pltpu.VMEM((k, c), F32), pltpu.SemaphoreType.DMA(())],
        compiler_params=_params("arbitrary", "arbitrary"),
        name="cmix",
    )(h, h, h, wfold, w_in, scale.reshape(1, e), w_out)


def kernel(x, a_norm, a_w_in, a_v_gain, a_w_s, a_b_s, a_w_out, b_norm, b_w_in, b_w_mix, b_w_out,
           c_norm, c_w_in, c_w_mix, c_scale, c_w_out, final_norm):
    bsz, seq, d = x.shape
    assert (seq, d) == (SEQ, D_MODEL)
    e = D_INNER
    depth = 4
    norms = {0: a_norm, 1: b_norm, 2: c_norm}

    def gain_for(i):
        if i == depth:
            return final_norm
        return norms[i % 3][i // 3]

    xf = x.reshape(bsz * seq, d)
    h = rms_cast(xf, gain_for(0))
    for i in range(depth):
        kind, j = i % 3, i // 3
        if kind == 0:
            (u,) = proj(h, a_w_in, j, 0, e, None, BF16, tm=2048)
            gv, ssq, w_out = proj(h, a_w_in, j, e, e, "gelu", BF16, with_ssq=True, cast=(a_w_out, j))
            y = amix(h, a_w_in, j, 2 * e, u, gv, ssq, a_v_gain[j], a_w_s[j], a_b_s[j])
        elif kind == 1:
            (xb,) = proj(h, b_w_in, j, 0, e, None, BF16, tm=2048)
            sg, w_out = proj(h, b_w_in, j, e, e, "silu", BF16, cast=(b_w_out, j))
            wf = fold_channel_dft(b_w_mix, j)
            y = fftmix(xb, wf, sg, bsz)
        else:
            wfold = fold_pool_weights(c_w_in, c_w_mix, j)
            y, w_out = cmix(h, wfold, c_w_in, j, e, c_scale[j], c_w_out)
        last = i == depth - 1
        res = out_proj(y, w_out, xf, gain_for(i + 1), last)
        if last:
            return res.reshape(bsz, seq, d)
        xf, h = res
```

```python
import functools

import numpy as np
import jax
import jax.numpy as jnp
from jax import lax
from jax.experimental import pallas as pl
from jax.experimental.pallas import tpu as pltpu

D_MODEL = 2048
D_INNER = 4096
SEQ = 4096
CHUNK = 128
A_GROUPS = 8
A_GROUP_DIM = D_INNER // A_GROUPS
B_GROUPS = 8
B_GROUP_DIM = D_INNER // B_GROUPS
POOL_WINDOWS = (2, 4, 8, 16)
C_GROUPS = len(POOL_WINDOWS)
assert POOL_WINDOWS == tuple(2 << g for g in range(C_GROUPS))
C_GROUP_DIM = D_INNER // C_GROUPS
EPS = 1e-6
HALO = 16

FFT_SLABS = 16
FFT_SLAB = SEQ // FFT_SLABS
BF16_ROWS = 16

V7X_VMEM_BYTES = 64 * 1024 * 1024
VMEM_LIMIT = V7X_VMEM_BYTES - 8 * 1024 * 1024

F32 = jnp.float32
BF16 = jnp.bfloat16


def _params(*sem):
    return pltpu.CompilerParams(dimension_semantics=sem, vmem_limit_bytes=VMEM_LIMIT)


def _norm_kernel(x_ref, g_ref, h_ref):
    x = x_ref[...]
    r = lax.rsqrt(jnp.mean(x * x, axis=-1, keepdims=True) + EPS)
    h_ref[...] = (x * r * g_ref[...]).astype(h_ref.dtype)


def rms_cast(x, g, tm=1024):
    m, d = x.shape
    return pl.pallas_call(
        _norm_kernel,
        grid=(m // tm,),
        in_specs=[pl.BlockSpec((tm, d), lambda i: (i, 0)),
                  pl.BlockSpec((1, d), lambda i: (0, 0))],
        out_specs=pl.BlockSpec((tm, d), lambda i: (i, 0)),
        out_shape=jax.ShapeDtypeStruct((m, d), BF16),
        compiler_params=_params("parallel"),
        name="rms_cast",
    )(x, g.reshape(1, d))


GELU_C = 0.7978845608028654
GELU_A = 0.044715


def _act(x, act):
    if act is None:
        return x
    hx = 0.5 * x
    if act == "gelu":
        t = jnp.tanh(x * (GELU_C + (GELU_C * GELU_A) * (x * x)))
    else:
        t = jnp.tanh(hx)
    return hx + hx * t


def _proj_kernel(a_ref, w_hbm, *refs, act, with_ssq, with_cast, layer, col0, tn, n_tiles):
    refs = list(refs)
    cast_in = refs.pop(0) if with_cast else None
    o_ref = refs.pop(0)
    ss_ref = refs.pop(0) if with_ssq else None
    cast_out = refs.pop(0) if with_cast else None
    wbf_ref, stage_ref, sem = refs

    def cast():
        wbf_ref[...] = stage_ref[...].astype(BF16)

    _staged_weight_cast(w_hbm, stage_ref, sem, layer, col0, tn, pl.program_id(0), n_tiles,
                        pl.program_id(1) == 0, cast)

    val = _act(jnp.dot(a_ref[...], wbf_ref[...], preferred_element_type=F32), act)
    o_ref[...] = val.astype(o_ref.dtype)
    if with_ssq:
        ss_ref[...] = jnp.sum(val * val, axis=-1, keepdims=True)
    if with_cast:
        cast_out[...] = cast_in[...].astype(cast_out.dtype)


def proj(a, w, layer, col0, ncols, act, out_dtype, with_ssq=False, cast=None, tm=1024, tn=1024):
    m, k = a.shape
    n_tiles = ncols // tn
    n_m = m // tm
    in_specs = [pl.BlockSpec((tm, k), lambda j, i: (i, 0)),
                pl.BlockSpec(memory_space=pl.ANY)]
    args = [a, w]
    out_specs = [pl.BlockSpec((tm, tn), lambda j, i: (i, j))]
    out_shape = [jax.ShapeDtypeStruct((m, ncols), out_dtype)]
    if with_ssq:
        out_specs.append(pl.BlockSpec((None, tm, 1), lambda j, i: (j, i, 0)))
        out_shape.append(jax.ShapeDtypeStruct((n_tiles, m, 1), F32))
    if cast is not None:
        cw, cl = cast
        _, ce, cd = cw.shape
        slab = ce // (n_tiles * n_m)
        assert slab * n_tiles * n_m == ce and slab % BF16_ROWS == 0
        in_specs.append(pl.BlockSpec((None, slab, cd), lambda j, i: (cl, j * n_m + i, 0)))
        args.append(cw)
        out_specs.append(pl.BlockSpec((slab, cd), lambda j, i: (j * n_m + i, 0)))
        out_shape.append(jax.ShapeDtypeStruct((ce, cd), BF16))
    return pl.pallas_call(
        functools.partial(_proj_kernel, act=act, with_ssq=with_ssq, with_cast=cast is not None,
                          layer=layer, col0=col0, tn=tn, n_tiles=n_tiles),
        grid=(n_tiles, n_m),
        in_specs=in_specs,
        out_specs=out_specs,
        out_shape=out_shape,
        scratch_shapes=[pltpu.VMEM((k, tn), BF16), pltpu.VMEM((k, tn), F32), pltpu.SemaphoreType.DMA(())],
        compiler_params=_params("arbitrary", "arbitrary"),
        name="proj_" + str(act),
    )(*args)


def _staged_weight_cast(w_hbm, stage_ref, sem, layer, col0, ncols, blk, n_blk, first, cast):
    def copy(b):
        start = pl.multiple_of(col0 + b * ncols, ncols)
        return pltpu.make_async_copy(w_hbm.at[layer, :, pl.ds(start, ncols)], stage_ref, sem)

    @pl.when(first & (blk == 0))
    def _():
        copy(blk).start()

    @pl.when(first)
    def _():
        copy(blk).wait()
        cast()

    @pl.when(first & (blk + 1 < n_blk))
    def _():
        copy(blk + 1).start()


def _out_kernel(y_ref, w_ref, x_ref, g_ref, *refs, n_tiles, tn, last):
    if last:
        o_ref, xrow = refs
    else:
        xo_ref, o_ref, xrow = refs
    n = pl.program_id(1)
    xn = x_ref[...] + jnp.dot(y_ref[...], w_ref[...], preferred_element_type=F32)
    if not last:
        xo_ref[...] = xn
    xrow[n] = xn

    @pl.when(n == n_tiles - 1)
    def _():
        ss = None
        for j in range(n_tiles):
            v = xrow[j]
            s = jnp.sum(v * v, axis=-1, keepdims=True)
            ss = s if ss is None else ss + s
        r = lax.rsqrt(ss * (1.0 / (n_tiles * tn)) + EPS)
        for j in range(n_tiles):
            cols = slice(j * tn, (j + 1) * tn)
            o_ref[:, cols] = (xrow[j] * r * g_ref[:, cols]).astype(o_ref.dtype)


def out_proj(y, w, x, g, last, tm=1024, tn=512):
    m, k = y.shape
    d = w.shape[1]
    n_tiles = d // tn
    in_specs = [pl.BlockSpec((tm, k), lambda i, n: (i, 0)),
                pl.BlockSpec((k, tn), lambda i, n: (0, n)),
                pl.BlockSpec((tm, tn), lambda i, n: (i, n)),
                pl.BlockSpec((1, d), lambda i, n: (0, 0))]
    row_spec = pl.BlockSpec((tm, d), lambda i, n: (i, 0))
    if last:
        out_specs = row_spec
        out_shape = jax.ShapeDtypeStruct((m, d), F32)
    else:
        out_specs = [pl.BlockSpec((tm, tn), lambda i, n: (i, n)), row_spec]
        out_shape = [jax.ShapeDtypeStruct((m, d), F32), jax.ShapeDtypeStruct((m, d), BF16)]
    return pl.pallas_call(
        functools.partial(_out_kernel, n_tiles=n_tiles, tn=tn, last=last),
        grid=(m // tm, n_tiles),
        in_specs=in_specs,
        out_specs=out_specs,
        out_shape=out_shape,
        scratch_shapes=[pltpu.VMEM((n_tiles, tm, tn), F32)],
        compiler_params=_params("parallel", "arbitrary"),
        name="out_proj",
    )(y, w, x, g.reshape(1, d))


AMIX_GROUPS_PER_STEP = 2


def _amix_kernel(h_ref, w_hbm, u_ref, gv_ref, ss_ref, gain_ref, ws_ref, b_ref, o_ref, wbf_ref, t_ref,
                 stage_ref, sem, *, tm, layer, gate_col0):
    i, s = pl.program_id(0), pl.program_id(1)
    gps, c = AMIX_GROUPS_PER_STEP, A_GROUP_DIM

    def cast():
        for gg in range(gps):
            wbf_ref[s * gps + gg] = stage_ref[:, gg * c:(gg + 1) * c].astype(BF16)

    _staged_weight_cast(w_hbm, stage_ref, sem, layer, gate_col0, gps * c, s, A_GROUPS // gps, i == 0, cast)

    ssq = ss_ref[0]
    for t in range(1, ss_ref.shape[0]):
        ssq = ssq + ss_ref[t]
    r = lax.rsqrt(ssq * (1.0 / D_INNER) + EPS)
    for gg in range(gps):
        cols = slice(gg * c, (gg + 1) * c)
        vn = (gv_ref[:, cols].astype(F32) * r * gain_ref[:, cols]).astype(BF16)
        wg = ws_ref[gg]
        bias = b_ref[gg]
        for ch in range(tm // CHUNK):
            rows = slice(ch * CHUNK, (ch + 1) * CHUNK)
            sv = jnp.dot(wg, vn[rows], preferred_element_type=F32) + bias
            t_ref[rows, cols] = _act(u_ref[rows, cols].astype(F32), "gelu") * sv
        sg = _act(jnp.dot(h_ref[...], wbf_ref[s * gps + gg], preferred_element_type=F32), "silu")
        o_ref[:, cols] = (t_ref[:, cols] * sg).astype(o_ref.dtype)


def amix(h, w_in, layer, gate_col0, u, gv, ssq, gain, w_s, b_s, tm=1024):
    m, e = u.shape
    k = h.shape[1]
    gps = AMIX_GROUPS_PER_STEP
    c = gps * A_GROUP_DIM
    n_steps = A_GROUPS // gps
    n_ss = ssq.shape[0]
    grp = pl.BlockSpec((tm, c), lambda i, s: (i, s))
    return pl.pallas_call(
        functools.partial(_amix_kernel, tm=tm, layer=layer, gate_col0=gate_col0),
        grid=(m // tm, n_steps),
        in_specs=[pl.BlockSpec((tm, k), lambda i, s: (i, 0)),
                  pl.BlockSpec(memory_space=pl.ANY),
                  grp, grp,
                  pl.BlockSpec((n_ss, tm, 1), lambda i, s: (0, i, 0)),
                  pl.BlockSpec((1, c), lambda i, s: (0, s)),
                  pl.BlockSpec((gps, CHUNK, CHUNK), lambda i, s: (s, 0, 0)),
                  pl.BlockSpec((gps, CHUNK, 1), lambda i, s: (s, 0, 0))],
        out_specs=grp,
        out_shape=jax.ShapeDtypeStruct((m, e), BF16),
        scratch_shapes=[pltpu.VMEM((A_GROUPS, k, A_GROUP_DIM), BF16), pltpu.VMEM((tm, c), F32),
                        pltpu.VMEM((k, c), F32), pltpu.SemaphoreType.DMA(())],
        compiler_params=_params("arbitrary", "arbitrary"),
        name="amix",
    )(h, w_in, u, gv, ssq, gain.reshape(1, e), w_s.astype(BF16), b_s.reshape(A_GROUPS, CHUNK, 1))


def _dft_tables(n):
    idx = jnp.arange(n, dtype=jnp.int32)
    ang = ((idx[:, None] * idx[None, :]) % n).astype(F32) * (2.0 * jnp.pi / n)
    return jnp.cos(ang), jnp.sin(ang)


def _fold_kernel(cs_ref, w_ref, o_ref, *, scale):
    acc = jnp.dot(cs_ref[...], w_ref[...].astype(BF16), preferred_element_type=F32)
    o_ref[...] = (acc * scale).astype(o_ref.dtype)


def fold_channel_dft(w_mix, layer):
    _, g, c, _ = w_mix.shape
    cc, sc = _dft_tables(c)
    cs = jnp.concatenate([cc, sc], axis=0).astype(BF16)
    scale = float((SEQ * c) ** -0.5)
    return pl.pallas_call(
        functools.partial(_fold_kernel, scale=scale),
        grid=(g,),
        in_specs=[pl.BlockSpec((2 * c, c), lambda i: (0, 0)),
                  pl.BlockSpec((None, None, c, c), lambda i: (layer, i, 0, 0))],
        out_specs=pl.BlockSpec((None, 2 * c, c), lambda i: (i, 0, 0)),
        out_shape=jax.ShapeDtypeStruct((g, 2 * c, c), BF16),
        compiler_params=_params("parallel"),
        name="fold_channel_dft",
    )(cs, w_mix)


def _slab_dft_tables():
    k1 = jnp.arange(FFT_SLABS // 2 + 1, dtype=jnp.int32)[:, None, None]
    j = jnp.arange(FFT_SLAB, dtype=jnp.int32)[None, :, None]
    n2 = jnp.arange(FFT_SLAB, dtype=jnp.int32)[None, None, :]
    a1 = ((k1 * n2) % SEQ).astype(F32) * (2.0 * jnp.pi / SEQ)
    a2 = ((j * n2) % FFT_SLAB).astype(F32) * (2.0 * jnp.pi / FFT_SLAB)
    c1, s1, c2, s2 = jnp.cos(a1), jnp.sin(a1), jnp.cos(a2), jnp.sin(a2)
    cos, sin = c1 * c2 - s1 * s2, s1 * c2 + c1 * s2
    top = jnp.concatenate([cos, sin], axis=2)
    bot = jnp.concatenate([-sin, cos], axis=2)
    return jnp.concatenate([top, bot], axis=1).astype(BF16)


def _interleave_table():
    k1, r = np.divmod(np.arange(FFT_SLAB), FFT_SLABS)
    k2 = np.where(k1 <= FFT_SLABS // 2, r, FFT_SLABS - 1 - r)
    p = np.zeros((FFT_SLAB, FFT_SLAB), np.float32)
    p[FFT_SLABS * k2 + k1, np.arange(FFT_SLAB)] = 1.0
    return jnp.asarray(p, dtype=BF16)


_STAGE1_BLOCKS = ([(0, 0)] + [(k1, part) for k1 in range(1, FFT_SLABS // 2) for part in (0, 1)]
                  + [(FFT_SLABS // 2, 0)])


def _stage1_table():
    k1 = jnp.asarray([b[0] for b in _STAGE1_BLOCKS], jnp.int32)[:, None]
    is_im = jnp.asarray([b[1] for b in _STAGE1_BLOCKS], jnp.int32)[:, None]
    m = jnp.arange(FFT_SLABS, dtype=jnp.int32)[None, :]
    ang = ((k1 * m) % FFT_SLABS).astype(F32) * (2.0 * jnp.pi / FFT_SLABS)
    coef = jnp.where(is_im == 1, -jnp.sin(ang), jnp.cos(ang))
    return jnp.kron(coef, jnp.eye(BF16_ROWS, dtype=F32)).astype(BF16)


def _fftmix_kernel(x_ref, m1_ref, g_ref, wf_ref, p_ref, sg_ref, o_ref, a_ref, ym_ref, *, tc):
    half = FFT_SLABS // 2
    c = B_GROUP_DIM

    for grp in range(FFT_SLAB // BF16_ROWS):
        r0 = grp * BF16_ROWS
        xg = jnp.concatenate([x_ref[m * FFT_SLAB + r0:m * FFT_SLAB + r0 + BF16_ROWS, :]
                              for m in range(FFT_SLABS)], axis=0)
        s1 = jnp.dot(m1_ref[...], xg, preferred_element_type=F32).astype(BF16)
        for blk, (k1, part) in enumerate(_STAGE1_BLOCKS):
            a_ref[k1, part * FFT_SLAB + r0:part * FFT_SLAB + r0 + BF16_ROWS, :] = (
                s1[blk * BF16_ROWS:(blk + 1) * BF16_ROWS])

    for k1 in range(half + 1):
        if k1 % half == 0:
            y = jnp.dot(g_ref[k1, :, :FFT_SLAB], a_ref[k1, :FFT_SLAB, :], preferred_element_type=F32)
        else:
            y = jnp.dot(g_ref[k1], a_ref[k1], preferred_element_type=F32)
        y = y.astype(BF16)
        for g in range(tc // c):
            cols = slice(g * c, (g + 1) * c)
            p_re = jnp.dot(y[:FFT_SLAB, cols], wf_ref[g, :c, :], preferred_element_type=F32)
            p_im = jnp.dot(y[FFT_SLAB:, cols], wf_ref[g, c:, :], preferred_element_type=F32)
            fwd = (p_re + p_im).astype(BF16)
            for k3 in range(FFT_SLABS):
                ym_ref[k3, k1 * FFT_SLABS:(k1 + 1) * FFT_SLABS, cols] = (
                    fwd[k3 * FFT_SLABS:(k3 + 1) * FFT_SLABS])
            if k1 % half != 0:
                mir = (p_re - p_im).astype(BF16)
                km = FFT_SLABS - k1
                for k3 in range(FFT_SLABS):
                    src = FFT_SLABS - 1 - k3
                    ym_ref[k3, km * FFT_SLABS:(km + 1) * FFT_SLABS, cols] = (
                        mir[src * FFT_SLABS:(src + 1) * FFT_SLABS])

    for k3 in range(FFT_SLABS):
        rows = slice(k3 * FFT_SLAB, (k3 + 1) * FFT_SLAB)
        z = jnp.dot(p_ref[...], ym_ref[k3], preferred_element_type=F32)
        o_ref[rows, :] = (z * sg_ref[rows, :].astype(F32)).astype(o_ref.dtype)


def fftmix(xb, wf, sg, bsz, tc=512):
    e = xb.shape[1]
    gpt = tc // B_GROUP_DIM
    tile = pl.BlockSpec((None, SEQ, tc), lambda b, j: (b, 0, j))
    once = pl.Buffered(1)
    out = pl.pallas_call(
        functools.partial(_fftmix_kernel, tc=tc),
        grid=(bsz, e // tc),
        in_specs=[tile,
                  pl.BlockSpec((FFT_SLAB, FFT_SLAB), lambda b, j: (0, 0), pipeline_mode=once),
                  pl.BlockSpec((FFT_SLABS // 2 + 1, 2 * FFT_SLAB, 2 * FFT_SLAB), lambda b, j: (0, 0, 0),
                               pipeline_mode=once),
                  pl.BlockSpec((gpt, 2 * B_GROUP_DIM, B_GROUP_DIM), lambda b, j: (j, 0, 0)),
                  pl.BlockSpec((FFT_SLAB, FFT_SLAB), lambda b, j: (0, 0), pipeline_mode=once),
                  tile],
        out_specs=tile,
        out_shape=jax.ShapeDtypeStruct((bsz, SEQ, e), BF16),
        scratch_shapes=[pltpu.VMEM((FFT_SLABS // 2 + 1, 2 * FFT_SLAB, tc), BF16),
                        pltpu.VMEM((FFT_SLABS, FFT_SLAB, tc), BF16)],
        compiler_params=_params("parallel", "parallel"),
        name="fftmix",
    )(xb.reshape(bsz, SEQ, e), _stage1_table(), _slab_dft_tables(), wf, _interleave_table(),
      sg.reshape(bsz, SEQ, e))
    return out.reshape(bsz * SEQ, e)


POOL_ROWS = 128
POOL_K = 256
FOLD_ROWS = 1024


def _cfold_kernel(wc_ref, wm_ref, o_ref, wmbf_ref):
    @pl.when(pl.program_id(1) == 0)
    def _():
        wmbf_ref[...] = wm_ref[...].astype(BF16)

    o_ref[...] = jnp.dot(wc_ref[...].astype(BF16), wmbf_ref[...],
                         preferred_element_type=F32).astype(o_ref.dtype)


def fold_pool_weights(w_in, w_mix, layer):
    _, d, _ = w_in.shape
    c = C_GROUP_DIM
    return pl.pallas_call(
        _cfold_kernel,
        grid=(C_GROUPS, d // FOLD_ROWS),
        in_specs=[pl.BlockSpec((None, FOLD_ROWS, c), lambda g, t: (layer, t, g)),
                  pl.BlockSpec((None, None, c, c), lambda g, t: (layer, g, 0, 0))],
        out_specs=pl.BlockSpec((None, FOLD_ROWS, c), lambda g, t: (g, t, 0)),
        out_shape=jax.ShapeDtypeStruct((C_GROUPS, d, c), BF16),
        scratch_shapes=[pltpu.VMEM((c, c), BF16)],
        compiler_params=_params("arbitrary", "arbitrary"),
        name="fold_pool_weights",
    )(w_in, w_mix)


def _cmix_kernel(prev_ref, cur_ref, next_ref, wf_ref, w_hbm, scale_ref, cast_in, o_ref, cast_out,
                 wgbf_ref, stage_ref, sem, *, tm, layer, gate_col0):
    g = pl.program_id(0)
    i = pl.program_id(1)
    cast_out[...] = cast_in[...].astype(cast_out.dtype)

    def cast():
        wgbf_ref[...] = stage_ref[...].astype(BF16)

    _staged_weight_cast(w_hbm, stage_ref, sem, layer, gate_col0, C_GROUP_DIM, g, C_GROUPS, i == 0, cast)

    win = jnp.left_shift(2, g)
    lo_off = win // 2
    hi_off = win - 1 - lo_off
    t0 = (i % (SEQ // tm)) * tm
    ext = jnp.concatenate([prev_ref[...], cur_ref[...], next_ref[...]], axis=0)
    sg = _act(jnp.dot(cur_ref[...], wgbf_ref[...], preferred_element_type=F32), "silu")
    pooled = []
    for j in range(tm // POOL_ROWS):
        start = min(j * POOL_ROWS, tm + 2 * HALO - POOL_K)
        t = t0 + j * POOL_ROWS + lax.broadcasted_iota(jnp.int32, (POOL_ROWS, 1), 0)
        u = t0 + (start - HALO) + lax.broadcasted_iota(jnp.int32, (1, POOL_K), 1)
        lo = jnp.maximum(t - lo_off, 0)
        hi = jnp.minimum(t + hi_off, SEQ - 1)
        band = jnp.where((u >= lo) & (u <= hi), 1.0, 0.0).astype(BF16)
        wsum = jnp.dot(band, ext[start:start + POOL_K], preferred_element_type=F32)
        cnt = (hi - lo + 1).astype(F32)
        rows = slice(j * POOL_ROWS, (j + 1) * POOL_ROWS)
        pooled.append((wsum / cnt - cur_ref[rows, :].astype(F32)).astype(BF16))
    hp = jnp.concatenate(pooled, axis=0)
    acc = jnp.dot(hp, wf_ref[...], preferred_element_type=F32)
    o_ref[...] = (acc * scale_ref[...] * sg).astype(o_ref.dtype)


def cmix(h, wfold, w_in, layer, gate_col0, scale, w_out, tm=1024):
    m, k = h.shape
    c = C_GROUP_DIM
    e = C_GROUPS * c
    hb = tm // HALO
    last_hb = m // HALO - 1
    n_m = m // tm
    _, ce, cd = w_out.shape
    slab = ce // (C_GROUPS * n_m)
    assert slab * C_GROUPS * n_m == ce and slab % BF16_ROWS == 0
    return pl.pallas_call(
        functools.partial(_cmix_kernel, tm=tm, layer=layer, gate_col0=gate_col0),
        grid=(C_GROUPS, m // tm),
        in_specs=[pl.BlockSpec((HALO, k), lambda g, i: (jnp.maximum(i * hb - 1, 0), 0)),
                  pl.BlockSpec((tm, k), lambda g, i: (i, 0)),
                  pl.BlockSpec((HALO, k), lambda g, i: (jnp.minimum((i + 1) * hb, last_hb), 0)),
                  pl.BlockSpec((None, k, c), lambda g, i: (g, 0, 0)),
                  pl.BlockSpec(memory_space=pl.ANY),
                  pl.BlockSpec((1, c), lambda g, i: (0, g)),
                  pl.BlockSpec((None, slab, cd), lambda g, i: (layer, g * n_m + i, 0))],
        out_specs=[pl.BlockSpec((tm, c), lambda g, i: (i, g)),
                   pl.BlockSpec((slab, cd), lambda g, i: (g * n_m + i, 0))],
        out_shape=[jax.ShapeDtypeStruct((m, e), BF16), jax.ShapeDtypeStruct((ce, cd), BF16)],
        scratch_shapes=[pltpu.VMEM((k, c), BF16), pltpu.VMEM((k, c), F32), pltpu.SemaphoreType.DMA(())],
        compiler_params=_params("arbitrary", "arbitrary"),
        name="cmix",
    )(h, h, h, wfold, w_in, scale.reshape(1, e), w_out)


def kernel(x, a_norm, a_w_in, a_v_gain, a_w_s, a_b_s, a_w_out, b_norm, b_w_in, b_w_mix, b_w_out,
           c_norm, c_w_in, c_w_mix, c_scale, c_w_out, final_norm):
    bsz, seq, d = x.shape
    assert (seq, d) == (SEQ, D_MODEL)
    e = D_INNER
    depth = 4
    norms = {0: a_norm, 1: b_norm, 2: c_norm}

    def gain_for(i):
        if i == depth:
            return final_norm
        return norms[i % 3][i // 3]

    xf = x.reshape(bsz * seq, d)
    h = rms_cast(xf, gain_for(0))
    for i in range(depth):
        kind, j = i % 3, i // 3
        if kind == 0:
            (u,) = proj(h, a_w_in, j, 0, e, None, BF16, tm=2048)
            gv, ssq, w_out = proj(h, a_w_in, j, e, e, "gelu", BF16, with_ssq=True, cast=(a_w_out, j))
            y = amix(h, a_w_in, j, 2 * e, u, gv, ssq, a_v_gain[j], a_w_s[j], a_b_s[j])
        elif kind == 1:
            (xb,) = proj(h, b_w_in, j, 0, e, None, BF16, tm=2048)
            sg, w_out = proj(h, b_w_in, j, e, e, "silu", BF16, cast=(b_w_out, j), tm=2048)
            wf = fold_channel_dft(b_w_mix, j)
            y = fftmix(xb, wf, sg, bsz)
        else:
            wfold = fold_pool_weights(c_w_in, c_w_mix, j)
            y, w_out = cmix(h, wfold, c_w_in, j, e, c_scale[j], c_w_out)
        last = i == depth - 1
        res = out_proj(y, w_out, xf, gain_for(i + 1), last)
        if last:
            return res.reshape(bsz, seq, d)
        xf, h = res
```
